```python
import jax, jax.numpy as jnp
from jax import lax
import numpy as np

D_MODEL = 1024
BATCH = 16
SEQ = 2048
DEPTH = 2

GRID_W = 64
CTX_LEN = 256
HEAD_DIM = 64
ATTN_WIDTH = D_MODEL // 2
N_Q_HEADS = ATTN_WIDTH // HEAD_DIM
N_KV_HEADS = max(1, N_Q_HEADS // 4)
KV_WIDTH = N_KV_HEADS * HEAD_DIM
CONV_WIDTH = D_MODEL - ATTN_WIDTH
CONV_KERNEL = 31
FFN_HIDDEN = ((8 * D_MODEL // 3 + 127) // 128) * 128
FFN_KERNEL = 3
Q_BLOCK = 128
ROPE_THETA = 10000.0
EPS = 1e-6
IN_WIDTH = ATTN_WIDTH + 2 * KV_WIDTH + 2 * CONV_WIDTH
MOD_WIDTH = 6 * D_MODEL

kernel_name = "hybrid_attn_conformer_dit_block"


def rms_norm(x, g):
    x32 = x.astype(jnp.float32)
    y = x32 * lax.rsqrt(jnp.mean(x32 * x32, axis=-1, keepdims=True) + EPS)
    return y.astype(x.dtype) * g


def layer_norm(x, g, b):
    x32 = x.astype(jnp.float32)
    mu = jnp.mean(x32, axis=-1, keepdims=True)
    var = jnp.mean(jnp.square(x32 - mu), axis=-1, keepdims=True)
    y = (x32 - mu) * lax.rsqrt(var + EPS)
    return y.astype(x.dtype) * g + b


def depthwise_conv(x, w, b):
    k = w.shape[0]
    y = lax.conv_general_dilated(
        x, w[:, None, :].astype(x.dtype), window_strides=(1,),
        padding=[(k // 2, k // 2)], dimension_numbers=('NWC', 'WIO', 'NWC'),
        feature_group_count=x.shape[-1])
    return y + b


def axial_rope_tables(rows, dtype):
    row = jnp.repeat(jnp.arange(rows, dtype=jnp.int32), GRID_W)
    col = jnp.tile(jnp.arange(GRID_W, dtype=jnp.int32), rows)
    pos = jnp.stack([row, col], axis=-1).astype(jnp.float32)
    n_freq = HEAD_DIM // 4
    inv_freq = ROPE_THETA ** (-jnp.arange(n_freq, dtype=jnp.float32) / n_freq)
    ang = pos[:, :, None] * inv_freq
    return jnp.cos(ang).astype(dtype), jnp.sin(ang).astype(dtype)


def apply_rope(x, cos, sin):
    b, s, h, _ = x.shape
    xr = x.reshape(b, s, h, 2, 2, HEAD_DIM // 4)
    x1, x2 = xr[..., 0, :], xr[..., 1, :]
    cs, sn = cos[None, :, None], sin[None, :, None]
    out = jnp.stack([x1 * cs - x2 * sn, x2 * cs + x1 * sn], axis=-2)
    return out.reshape(b, s, h, HEAD_DIM)


def modulation(cvec, w_mod, b_mod):
    return jnp.split(jax.nn.silu(cvec) @ w_mod + b_mod, 6, axis=-1)


def modulate(x, g, shift, scale):
    return rms_norm(x, g) * (1 + scale) + shift


def q_heads(h, w_in, q_g):
    b, l, _ = h.shape
    q = (h @ w_in[:, :ATTN_WIDTH]).reshape(b, l, N_Q_HEADS, HEAD_DIM)
    return rms_norm(q, q_g)


def kv_heads(h, w_in, k_g):
    b, l, _ = h.shape
    kv = h @ w_in[:, ATTN_WIDTH:ATTN_WIDTH + 2 * KV_WIDTH]
    k, v = jnp.split(kv, 2, axis=-1)
    k = rms_norm(k.reshape(b, l, N_KV_HEADS, HEAD_DIM), k_g)
    return k, v.reshape(b, l, N_KV_HEADS, HEAD_DIM)


def conv_input(h, w_in):
    return h @ w_in[:, ATTN_WIDTH + 2 * KV_WIDTH:]


def conformer_conv(u, dw, dwb, ln_g, ln_b):
    a, gate = jnp.split(u, 2, axis=-1)
    y = depthwise_conv(a * jax.nn.sigmoid(gate), dw, dwb)
    return jax.nn.silu(layer_norm(y, ln_g, ln_b))


def latent_attention(q, k, v, kc, vc):
    b, s = q.shape[:2]
    nb = s // Q_BLOCK
    grp = N_Q_HEADS // N_KV_HEADS
    scale = HEAD_DIM ** -0.5
    qb = q.reshape(b, nb, Q_BLOCK, N_KV_HEADS, grp, HEAD_DIM).transpose(1, 0, 2, 3, 4, 5)

    def one_block(qi):
        s_lat = jnp.einsum('bqkgd,bskd->bkgqs', qi, k)
        s_ctx = jnp.einsum('bqkgd,bckd->bkgqc', qi, kc)
        sc = jnp.concatenate([s_lat, s_ctx], axis=-1).astype(jnp.float32) * scale
        p = jax.nn.softmax(sc, axis=-1).astype(v.dtype)
        return (jnp.einsum('bkgqs,bskd->bqkgd', p[..., :s], v)
                + jnp.einsum('bkgqc,bckd->bqkgd', p[..., s:], vc))

    o = lax.map(one_block, qb)
    return o.transpose(1, 0, 2, 3, 4, 5).reshape(b, s, ATTN_WIDTH)


def context_attention(q, k, v):
    b, l = q.shape[:2]
    grp = N_Q_HEADS // N_KV_HEADS
    qg = q.reshape(b, l, N_KV_HEADS, grp, HEAD_DIM)
    sc = jnp.einsum('bqkgd,bckd->bkgqc', qg, k).astype(jnp.float32) * (HEAD_DIM ** -0.5)
    p = jax.nn.softmax(sc, axis=-1).astype(v.dtype)
    return jnp.einsum('bkgqc,bckd->bqkgd', p, v).reshape(b, l, ATTN_WIDTH)


def conv_glu_ffn(h, w_up, dw, dwb, w_down):
    gate, val = jnp.split(h @ w_up, 2, axis=-1)
    return (jax.nn.silu(depthwise_conv(gate, dw, dwb)) * val) @ w_down


def setup_inputs(seed: int = 0) -> dict:
    key = jax.random.key(seed)
    ks = jax.random.split(key, 24)
    f32 = jnp.float32
    n = lambda k, shape, s: jax.random.normal(k, shape, f32) * s
    return {
        "x": n(ks[0], (BATCH, SEQ, D_MODEL), 1.0),
        "c": n(ks[1], (BATCH, D_MODEL), 1.0),
        "ctx": n(ks[2], (BATCH, CTX_LEN, D_MODEL), 1.0),
        "c_ctx": n(ks[3], (D_MODEL,), 1.0),
        "w_mod": n(ks[4], (DEPTH, D_MODEL, MOD_WIDTH), 0.5 * D_MODEL ** -0.5),
        "b_mod": n(ks[5], (DEPTH, MOD_WIDTH), 0.02),
        "norm1_g": 1.0 + n(ks[6], (DEPTH, D_MODEL), 0.05),
        "norm2_g": 1.0 + n(ks[7], (DEPTH, D_MODEL), 0.05),
        "w_in": n(ks[8], (DEPTH, D_MODEL, IN_WIDTH), D_MODEL ** -0.5),
        "q_norm_g": 1.0 + n(ks[9], (DEPTH, HEAD_DIM), 0.05),
        "k_norm_g": 1.0 + n(ks[10], (DEPTH, HEAD_DIM), 0.05),
        "conv_dw": n(ks[11], (DEPTH, CONV_KERNEL, CONV_WIDTH), CONV_KERNEL ** -0.5),
        "conv_dw_b": n(ks[12], (DEPTH, CONV_WIDTH), 0.02),
        "conv_ln_g": 1.0 + n(ks[13], (DEPTH, CONV_WIDTH), 0.05),
        "conv_ln_b": n(ks[14], (DEPTH, CONV_WIDTH), 0.02),
        "w_out": n(ks[15], (DEPTH, D_MODEL, D_MODEL), D_MODEL ** -0.5),
        "ffn_w_up": n(ks[16], (DEPTH, D_MODEL, 2 * FFN_HIDDEN), D_MODEL ** -0.5),
        "ffn_dw": n(ks[17], (DEPTH, FFN_KERNEL, FFN_HIDDEN), FFN_KERNEL ** -0.5),
        "ffn_dw_b": n(ks[18], (DEPTH, FFN_HIDDEN), 0.02),
        "ffn_w_down": n(ks[19], (DEPTH, FFN_HIDDEN, D_MODEL), FFN_HIDDEN ** -0.5),
        "final_g": 1.0 + n(ks[20], (D_MODEL,), 0.05),
    }


def reference(x, c, ctx, c_ctx, w_mod, b_mod, norm1_g, norm2_g, w_in, q_norm_g, k_norm_g,
              conv_dw, conv_dw_b, conv_ln_g, conv_ln_b, w_out, ffn_w_up, ffn_dw, ffn_dw_b,
              ffn_w_down, final_g):
    n_tok = x.shape[1]
    ROWS = n_tok // GRID_W
    cos, sin = axial_rope_tables(ROWS, x.dtype)
    cx = ctx
    for l in range(DEPTH):
        last = l == DEPTH - 1
        sh1, sc1, g1, sh2, sc2, g2 = [m[:, None, :] for m in modulation(c, w_mod[l], b_mod[l])]
        csh1, csc1, cg1, csh2, csc2, cg2 = modulation(c_ctx, w_mod[l], b_mod[l])

        h = modulate(x, norm1_g[l], sh1, sc1)
        hc = modulate(cx, norm1_g[l], csh1, csc1)
        q = apply_rope(q_heads(h, w_in[l], q_norm_g[l]), cos, sin)
        k, v = kv_heads(h, w_in[l], k_norm_g[l])
        k = apply_rope(k, cos, sin)
        kc, vc = kv_heads(hc, w_in[l], k_norm_g[l])
        attn = latent_attention(q, k, v, kc, vc)
        conv = conformer_conv(conv_input(h, w_in[l]), conv_dw[l], conv_dw_b[l],
                              conv_ln_g[l], conv_ln_b[l])
        x = x + g1 * (jnp.concatenate([attn, conv], axis=-1) @ w_out[l])

        h2 = modulate(x, norm2_g[l], sh2, sc2)
        x = x + g2 * conv_glu_ffn(h2, ffn_w_up[l], ffn_dw[l], ffn_dw_b[l], ffn_w_down[l])

        if not last:
            qc = q_heads(hc, w_in[l], q_norm_g[l])
            attn_c = context_attention(qc, kc, vc)
            conv_c = conformer_conv(conv_input(hc, w_in[l]), conv_dw[l], conv_dw_b[l],
                                    conv_ln_g[l], conv_ln_b[l])
            cx = cx + cg1 * (jnp.concatenate([attn_c, conv_c], axis=-1) @ w_out[l])
            hc2 = modulate(cx, norm2_g[l], csh2, csc2)
            cx = cx + cg2 * conv_glu_ffn(hc2, ffn_w_up[l], ffn_dw[l], ffn_dw_b[l], ffn_w_down[l])
    return rms_norm(x, final_g)
```

```python
import functools

import jax
import jax.numpy as jnp
from jax import lax
from jax.experimental import pallas as pl
from jax.experimental.pallas import tpu as pltpu

F32 = jnp.float32
BF16 = jnp.bfloat16

D_MODEL = 1024
HEAD_DIM = 64
N_Q_HEADS = 8
N_KV_HEADS = 2
GROUP = N_Q_HEADS // N_KV_HEADS
ATTN_WIDTH = N_Q_HEADS * HEAD_DIM
KV_WIDTH = N_KV_HEADS * HEAD_DIM
QKV_WIDTH = ATTN_WIDTH + 2 * KV_WIDTH
CONV_WIDTH = D_MODEL - ATTN_WIDTH
CONV_KERNEL = 31
CONV_PAD = CONV_KERNEL // 2
FFN_HIDDEN = 2816
FFN_KERNEL = 3
GRID_W = 64
ROPE_THETA = 10000.0
EPS = 1e-6

LANES = 128
TILE = 256
CONV_HALO = 16
FFN_HALO = 8
FFN_CHUNK = 256
N_FFN_CHUNKS = FFN_HIDDEN // FFN_CHUNK
MOD_ROWS = 24
MOD_TN = 1024
VMEM_LIMIT = 56 * 1024 * 1024

NT_DIMS = (((1,), (1,)), ((), ()))


def _params(sem, vmem=VMEM_LIMIT):
    return pltpu.CompilerParams(dimension_semantics=sem, vmem_limit_bytes=vmem)


def _mod_kernel(c_ref, w_ref, b_ref, o_ref):
    c = c_ref[...]
    s = c * jax.nn.sigmoid(c)
    o_ref[0] = jnp.dot(s, w_ref[0], preferred_element_type=F32,
                       precision=lax.Precision.HIGHEST) + b_ref[0]


def _modulation(cc, w_mod, b_mod):
    depth = w_mod.shape[0]
    n = w_mod.shape[2]
    return pl.pallas_call(
        _mod_kernel,
        grid=(depth, n // MOD_TN),
        in_specs=[
            pl.BlockSpec((MOD_ROWS, D_MODEL), lambda l, j: (0, 0)),
            pl.BlockSpec((1, D_MODEL, MOD_TN), lambda l, j: (l, 0, j)),
            pl.BlockSpec((1, 1, MOD_TN), lambda l, j: (l, 0, j)),
        ],
        out_specs=pl.BlockSpec((1, MOD_ROWS, MOD_TN), lambda l, j: (l, 0, j)),
        out_shape=jax.ShapeDtypeStruct((depth, MOD_ROWS, n), F32),
        compiler_params=_params(("parallel", "parallel")),
        name="modulation",
    )(cc, w_mod, b_mod.reshape(depth, 1, n))


def _rms_mod(x, g, scale, shift):
    ms = jnp.mean(x * x, axis=-1, keepdims=True)
    return (x * lax.rsqrt(ms + EPS)) * g * (1.0 + scale) + shift


def _inproj_kernel(x_ref, mod_ref, g_ref, wqkvt_ref, wconv_ref, gq_ref, gk_ref, cos_ref, sin_ref,
                   q_ref, k_ref, v_ref, glu_ref):
    h = _rms_mod(x_ref[0], g_ref[...], mod_ref[0, 1:2, :], mod_ref[0, 0:1, :]).astype(BF16)

    u = jnp.dot(h, wconv_ref[...], preferred_element_type=F32)
    glu_ref[0] = u[:, :CONV_WIDTH] * jax.nn.sigmoid(u[:, CONV_WIDTH:])

    qkvt = lax.dot_general(wqkvt_ref[...], h, NT_DIMS, preferred_element_type=F32)
    cos = cos_ref[...]
    sin = sin_ref[...]

    def head(row0, gain, scale):
        blk = qkvt[row0:row0 + HEAD_DIM, :]
        ms = jnp.mean(blk * blk, axis=0, keepdims=True)
        n = blk * lax.rsqrt(ms + EPS) * gain
        swapped = jnp.concatenate([n[16:32], n[0:16], n[48:64], n[32:48]], axis=0)
        return (n * cos + swapped * sin) * scale

    gq = gq_ref[...]
    gk = gk_ref[...]
    qt = jnp.concatenate([head(i * HEAD_DIM, gq, HEAD_DIM ** -0.5) for i in range(N_Q_HEADS)], axis=0)
    q_ref[0] = qt.T.astype(BF16)
    k0 = head(ATTN_WIDTH, gk, 1.0)
    k1 = head(ATTN_WIDTH + HEAD_DIM, gk, 1.0)
    k_ref[0] = jnp.concatenate([k0, k0, k1, k1], axis=0).T.astype(BF16)
    v0 = qkvt[ATTN_WIDTH + KV_WIDTH:ATTN_WIDTH + KV_WIDTH + HEAD_DIM, :]
    v1 = qkvt[ATTN_WIDTH + KV_WIDTH + HEAD_DIM:QKV_WIDTH, :]
    v_ref[0] = jnp.concatenate([v0, v0, v1, v1], axis=0).T.astype(BF16)


def _inproj(xs, mods, g, wqkvt, wconv, gq, gk, cos_t, sin_t, n_latent_tiles):
    batch, tokens, _ = xs.shape
    n_tiles = tokens // TILE

    def mod_row(b, t):
        return (jnp.where(t < n_latent_tiles, b, batch), 0, 0)

    const2 = lambda b, t: (0, 0)
    return pl.pallas_call(
        _inproj_kernel,
        grid=(batch, n_tiles),
        in_specs=[
            pl.BlockSpec((1, TILE, D_MODEL), lambda b, t: (b, t, 0)),
            pl.BlockSpec((1, 6, D_MODEL), mod_row),
            pl.BlockSpec((1, D_MODEL), const2),
            pl.BlockSpec((QKV_WIDTH, D_MODEL), const2),
            pl.BlockSpec((D_MODEL, 2 * CONV_WIDTH), const2),
            pl.BlockSpec((HEAD_DIM, TILE), const2),
            pl.BlockSpec((HEAD_DIM, TILE), const2),
            pl.BlockSpec((HEAD_DIM, TILE), lambda b, t: (0, t)),
            pl.BlockSpec((HEAD_DIM, TILE), lambda b, t: (0, t)),
        ],
        out_specs=[
            pl.BlockSpec((1, TILE, ATTN_WIDTH), lambda b, t: (b, t, 0)),
            pl.BlockSpec((1, TILE, 2 * KV_WIDTH), lambda b, t: (b, t, 0)),
            pl.BlockSpec((1, TILE, 2 * KV_WIDTH), lambda b, t: (b, t, 0)),
            pl.BlockSpec((1, TILE, CONV_WIDTH), lambda b, t: (b, t, 0)),
        ],
        out_shape=[
            jax.ShapeDtypeStruct((batch, tokens, ATTN_WIDTH), BF16),
            jax.ShapeDtypeStruct((batch, tokens, 2 * KV_WIDTH), BF16),
            jax.ShapeDtypeStruct((batch, tokens, 2 * KV_WIDTH), BF16),
            jax.ShapeDtypeStruct((batch, tokens, CONV_WIDTH), F32),
        ],
        compiler_params=_params(("parallel", "parallel")),
        name="inproj",
    )(xs, mods, g, wqkvt, wconv, gq, gk, cos_t, sin_t)


def _attend(q_ref, k, v, o_ref):
    upper = lax.broadcasted_iota(jnp.int32, (1, LANES), 1) >= HEAD_DIM
    for slab in range(GROUP // 2):
        qs = q_ref[0, :, slab * LANES:(slab + 1) * LANES]
        outs = []
        for e in range(2):
            keep = upper if e == 1 else jnp.logical_not(upper)
            qm = jnp.where(keep, qs, jnp.zeros_like(qs))
            s = lax.dot_general(qm, k, NT_DIMS, preferred_element_type=F32)
            m = jnp.max(s, axis=-1, keepdims=True)
            p = jnp.exp(s - m)
            l = jnp.sum(p, axis=-1, keepdims=True)
            o = jnp.dot(p.astype(BF16), v, preferred_element_type=F32)
            outs.append(o / l)
        o_ref[0, :, slab * LANES:(slab + 1) * LANES] = jnp.where(upper, outs[1], outs[0]).astype(BF16)


def _attn_kernel(q_ref, k_ref, v_ref, o_ref, *, n_latent_tiles, n_latent):
    t = pl.program_id(2)

    @pl.when(t < n_latent_tiles)
    def _():
        _attend(q_ref, k_ref[0], v_ref[0], o_ref)

    @pl.when(t >= n_latent_tiles)
    def _():
        _attend(q_ref, k_ref[0, n_latent:, :], v_ref[0, n_latent:, :], o_ref)


def _attention(q, k, v, n_latent_tiles, n_q_tiles):
    batch, tokens, _ = q.shape
    kernel = functools.partial(_attn_kernel, n_latent_tiles=n_latent_tiles,
                               n_latent=n_latent_tiles * TILE)
    return pl.pallas_call(
        kernel,
        grid=(batch, N_KV_HEADS, n_q_tiles),
        in_specs=[
            pl.BlockSpec((1, TILE, GROUP * HEAD_DIM), lambda b, g, t: (b, t, g)),
            pl.BlockSpec((1, tokens, LANES), lambda b, g, t: (b, 0, g)),
            pl.BlockSpec((1, tokens, LANES), lambda b, g, t: (b, 0, g)),
        ],
        out_specs=pl.BlockSpec((1, TILE, GROUP * HEAD_DIM), lambda b, g, t: (b, t, g)),
        out_shape=jax.ShapeDtypeStruct((batch, n_q_tiles * TILE, ATTN_WIDTH), BF16),
        compiler_params=_params(("parallel", "parallel", "parallel")),
        name="attention",
    )(q, k, v)


def _outproj_kernel(glu_ref, glu_prev_ref, glu_next_ref, attn_ref, x_ref, mod_ref, dw_ref, dwb_ref,
                    lng_ref, lnb_ref, wout_ref, o_ref, win_ref, *, n_latent_tiles):
    t = pl.program_id(1)
    has_prev = jnp.logical_and(t != 0, t != n_latent_tiles)
    has_next = jnp.logical_and(t != n_latent_tiles - 1, t != n_latent_tiles)
    win_ref[0:CONV_HALO, :] = jnp.where(has_prev, glu_prev_ref[0], 0.0)
    win_ref[CONV_HALO:CONV_HALO + TILE, :] = glu_ref[0]
    win_ref[CONV_HALO + TILE:, :] = jnp.where(has_next, glu_next_ref[0], 0.0)

    half = TILE // 2
    cols = []
    for c in range(CONV_WIDTH // LANES):
        rows = []
        for r in range(TILE // half):
            acc = jnp.zeros((half, LANES), F32)
            for j in range(CONV_KERNEL):
                start = r * half + j + CONV_HALO - CONV_PAD
                acc = acc + win_ref[start:start + half, c * LANES:(c + 1) * LANES] \
                    * dw_ref[j:j + 1, c * LANES:(c + 1) * LANES]
            rows.append(acc)
        cols.append(jnp.concatenate(rows, axis=0))
    y = jnp.concatenate(cols, axis=1) + dwb_ref[...]

    mu = jnp.mean(y, axis=-1, keepdims=True)
    d = y - mu
    var = jnp.mean(d * d, axis=-1, keepdims=True)
    z = d * lax.rsqrt(var + EPS) * lng_ref[...] + lnb_ref[...]
    conv = (z * jax.nn.sigmoid(z)).astype(BF16)

    proj = jnp.dot(attn_ref[0], wout_ref[:ATTN_WIDTH, :], preferred_element_type=F32) \
        + jnp.dot(conv, wout_ref[ATTN_WIDTH:, :], preferred_element_type=F32)
    o_ref[0] = x_ref[0] + mod_ref[0, 2:3, :] * proj


def _outproj(glu, attn, xs, mods, dw, dwb, lng, lnb, wout, n_latent_tiles, n_tiles):
    batch = xs.shape[0]
    halo_blocks = glu.shape[1] // CONV_HALO
    per_tile = TILE // CONV_HALO

    def mod_row(b, t):
        return (jnp.where(t < n_latent_tiles, b, batch), 0, 0)

    const2 = lambda b, t: (0, 0)
    kernel = functools.partial(_outproj_kernel, n_latent_tiles=n_latent_tiles)
    return pl.pallas_call(
        kernel,
        grid=(batch, n_tiles),
        in_specs=[
            pl.BlockSpec((1, TILE, CONV_WIDTH), lambda b, t: (b, t, 0)),
            pl.BlockSpec((1, CONV_HALO, CONV_WIDTH),
                         lambda b, t: (b, jnp.maximum(t * per_tile - 1, 0), 0)),
            pl.BlockSpec((1, CONV_HALO, CONV_WIDTH),
                         lambda b, t: (b, jnp.minimum((t + 1) * per_tile, halo_blocks - 1), 0)),
            pl.BlockSpec((1, TILE, ATTN_WIDTH), lambda b, t: (b, t, 0)),
            pl.BlockSpec((1, TILE, D_MODEL), lambda b, t: (b, t, 0)),
            pl.BlockSpec((1, 6, D_MODEL), mod_row),
            pl.BlockSpec((CONV_KERNEL + 1, CONV_WIDTH), const2),
            pl.BlockSpec((1, CONV_WIDTH), const2),
            pl.BlockSpec((1, CONV_WIDTH), const2),
            pl.BlockSpec((1, CONV_WIDTH), const2),
            pl.BlockSpec((D_MODEL, D_MODEL), const2),
        ],
        out_specs=pl.BlockSpec((1, TILE, D_MODEL), lambda b, t: (b, t, 0)),
        out_shape=jax.ShapeDtypeStruct((batch, n_tiles * TILE, D_MODEL), F32),
        scratch_shapes=[pltpu.VMEM((TILE + 2 * CONV_HALO, CONV_WIDTH), F32)],
        compiler_params=_params(("parallel", "parallel")),
        name="outproj",
    )(glu, glu, glu, attn, xs, mods, dw, dwb, lng, lnb, wout)


def _ffn_kernel(x_ref, x_prev_ref, x_next_ref, mod_ref, g_ref, wg_ref, wv_ref, wd_ref, dw_ref, dwb_ref,
                fg_ref, o_ref, xw_ref, gate_ref, acc_ref, *, n_latent_tiles, final_norm):
    t = pl.program_id(1)
    has_prev = jnp.logical_and(t != 0, t != n_latent_tiles)
    has_next = jnp.logical_and(t != n_latent_tiles - 1, t != n_latent_tiles)

    xw_ref[0:TILE, :] = x_ref[0]
    xw_ref[TILE:TILE + FFN_HALO, :] = x_next_ref[0]
    xw_ref[TILE + FFN_HALO:, :] = x_prev_ref[0]
    h = _rms_mod(xw_ref[...], g_ref[...], mod_ref[0, 4:5, :], mod_ref[0, 3:4, :]).astype(BF16)
    h_tile = h[0:TILE]
    acc_ref[...] = jnp.zeros_like(acc_ref)

    def chunk(c, carry):
        gate = jnp.dot(h, wg_ref[c], preferred_element_type=F32)
        val = jnp.dot(h_tile, wv_ref[c], preferred_element_type=F32)
        gate_ref[0:FFN_HALO, :] = jnp.where(has_prev, gate[TILE + FFN_HALO:], 0.0)
        gate_ref[FFN_HALO:FFN_HALO + TILE, :] = gate[0:TILE]
        gate_ref[FFN_HALO + TILE:, :] = jnp.where(has_next, gate[TILE:TILE + FFN_HALO], 0.0)
        dw = dw_ref[c]
        y = dwb_ref[c]
        for j in range(FFN_KERNEL):
            start = FFN_HALO - FFN_KERNEL // 2 + j
            y = y + gate_ref[start:start + TILE, :] * dw[j:j + 1, :]
        act = (y * jax.nn.sigmoid(y) * val).astype(BF16)
        acc_ref[...] += jnp.dot(act, wd_ref[c], preferred_element_type=F32)
        return carry

    lax.fori_loop(0, N_FFN_CHUNKS, chunk, 0)
    out = x_ref[0] + mod_ref[0, 5:6, :] * acc_ref[...]
    if final_norm:
        ms = jnp.mean(out * out, axis=-1, keepdims=True)
        out = out * lax.rsqrt(ms + EPS) * fg_ref[...]
    o_ref[0] = out


def _ffn(x1, mods, g, wg, wv, wd, dw, dwb, fg, n_latent_tiles, final_norm):
    batch, tokens, _ = x1.shape
    n_tiles = tokens // TILE
    halo_blocks = tokens // FFN_HALO
    per_tile = TILE // FFN_HALO

    def mod_row(b, t):
        return (jnp.where(t < n_latent_tiles, b, batch), 0, 0)

    const2 = lambda b, t: (0, 0)
    const3 = lambda b, t: (0, 0, 0)
    kernel = functools.partial(_ffn_kernel, n_latent_tiles=n_latent_tiles, final_norm=final_norm)
    return pl.pallas_call(
        kernel,
        grid=(batch, n_tiles),
        in_specs=[
            pl.BlockSpec((1, TILE, D_MODEL), lambda b, t: (b, t, 0)),
            pl.BlockSpec((1, FFN_HALO, D_MODEL),
                         lambda b, t: (b, jnp.maximum(t * per_tile - 1, 0), 0)),
            pl.BlockSpec((1, FFN_HALO, D_MODEL),
                         lambda b, t: (b, jnp.minimum((t + 1) * per_tile, halo_blocks - 1), 0)),
            pl.BlockSpec((1, 6, D_MODEL), mod_row),
            pl.BlockSpec((1, D_MODEL), const2),
            pl.BlockSpec((N_FFN_CHUNKS, D_MODEL, FFN_CHUNK), const3),
            pl.BlockSpec((N_FFN_CHUNKS, D_MODEL, FFN_CHUNK), const3),
            pl.BlockSpec((N_FFN_CHUNKS, FFN_CHUNK, D_MODEL), const3),
            pl.BlockSpec((N_FFN_CHUNKS, FFN_KERNEL, FFN_CHUNK), const3),
            pl.BlockSpec((N_FFN_CHUNKS, 1, FFN_CHUNK), const3),
            pl.BlockSpec((1, D_MODEL), const2),
        ],
        out_specs=pl.BlockSpec((1, TILE, D_MODEL), lambda b, t: (b, t, 0)),
        out_shape=jax.ShapeDtypeStruct((batch, tokens, D_MODEL), F32),
        scratch_shapes=[
            pltpu.VMEM((TILE + 2 * FFN_HALO, D_MODEL), F32),
            pltpu.VMEM((TILE + 2 * FFN_HALO, FFN_CHUNK), F32),
            pltpu.VMEM((TILE, D_MODEL), F32),
        ],
        compiler_params=_params(("parallel", "parallel")),
        name="ffn",
    )(x1, x1, x1, mods, g, wg, wv, wd, dw, dwb, fg)


def _rope_tables(n_latent, n_ctx):
    pos = jnp.arange(n_latent, dtype=jnp.int32)
    rc = jnp.stack([pos // GRID_W, pos % GRID_W], axis=0).astype(F32)
    n_freq = HEAD_DIM // 4
    inv_freq = ROPE_THETA ** (-jnp.arange(n_freq, dtype=F32) / n_freq)
    ang = rc[:, None, :] * inv_freq[None, :, None]
    cos, sin = jnp.cos(ang), jnp.sin(ang)
    cos_t = jnp.concatenate([cos, cos], axis=1).reshape(HEAD_DIM, n_latent)
    sin_t = jnp.concatenate([-sin, sin], axis=1).reshape(HEAD_DIM, n_latent)
    cos_t = jnp.concatenate([cos_t, jnp.ones((HEAD_DIM, n_ctx), F32)], axis=1)
    sin_t = jnp.concatenate([sin_t, jnp.zeros((HEAD_DIM, n_ctx), F32)], axis=1)
    return cos_t, sin_t


def _chunked_cols(w):
    rows = w.shape[0]
    return w.reshape(rows, N_FFN_CHUNKS, FFN_CHUNK).transpose(1, 0, 2)


def kernel(x, c, ctx, c_ctx, w_mod, b_mod, norm1_g, norm2_g, w_in, q_norm_g, k_norm_g,
           conv_dw, conv_dw_b, conv_ln_g, conv_ln_b, w_out, ffn_w_up, ffn_dw, ffn_dw_b,
           ffn_w_down, final_g):
    batch, seq, _ = x.shape
    n_ctx = ctx.shape[1]
    depth = w_mod.shape[0]
    assert seq % TILE == 0 and n_ctx == TILE and batch + 1 <= MOD_ROWS
    n_latent_tiles = seq // TILE
    n_tiles = n_latent_tiles + 1

    cc = jnp.concatenate([c, c_ctx[None, :], jnp.zeros((MOD_ROWS - batch - 1, D_MODEL), F32)], axis=0)
    mods = _modulation(cc, w_mod, b_mod).reshape(depth, MOD_ROWS, 6, D_MODEL)
    cos_t, sin_t = _rope_tables(seq, n_ctx)
    xs = jnp.concatenate([x, ctx], axis=1)

    out = None
    for l in range(depth):
        last = l == depth - 1
        wqkvt = w_in[l, :, :QKV_WIDTH].T.astype(BF16)
        wconv = w_in[l, :, QKV_WIDTH:].astype(BF16)
        gq = jnp.broadcast_to(q_norm_g[l][:, None], (HEAD_DIM, TILE))
        gk = jnp.broadcast_to(k_norm_g[l][:, None], (HEAD_DIM, TILE))
        q, k, v, glu = _inproj(xs, mods[l], norm1_g[l][None, :], wqkvt, wconv, gq, gk, cos_t, sin_t,
                               n_latent_tiles)
        n_out_tiles = n_latent_tiles if last else n_tiles
        attn = _attention(q, k, v, n_latent_tiles, n_out_tiles)
        dw = jnp.concatenate([conv_dw[l], jnp.zeros((1, CONV_WIDTH), F32)], axis=0)
        x1 = _outproj(glu, attn, xs, mods[l], dw, conv_dw_b[l][None, :], conv_ln_g[l][None, :],
                      conv_ln_b[l][None, :], w_out[l].astype(BF16), n_latent_tiles, n_out_tiles)
        wg = _chunked_cols(ffn_w_up[l, :, :FFN_HIDDEN]).astype(BF16)
        wv = _chunked_cols(ffn_w_up[l, :, FFN_HIDDEN:]).astype(BF16)
        wd = ffn_w_down[l].reshape(N_FFN_CHUNKS, FFN_CHUNK, D_MODEL).astype(BF16)
        fdw = _chunked_cols(ffn_dw[l])
        fdwb = ffn_dw_b[l].reshape(N_FFN_CHUNKS, 1, FFN_CHUNK)
        xs = _ffn(x1, mods[l], norm2_g[l][None, :], wg, wv, wd, fdw, fdwb, final_g[None, :],
                  n_latent_tiles, last)
    return xs
```

```python
import functools

import jax
import jax.numpy as jnp
from jax import lax
from jax.experimental import pallas as pl
from jax.experimental.pallas import tpu as pltpu

F32 = jnp.float32
BF16 = jnp.bfloat16

D_MODEL = 1024
HEAD_DIM = 64
N_Q_HEADS = 8
N_KV_HEADS = 2
GROUP = N_Q_HEADS // N_KV_HEADS
ATTN_WIDTH = N_Q_HEADS * HEAD_DIM
KV_WIDTH = N_KV_HEADS * HEAD_DIM
QKV_WIDTH = ATTN_WIDTH + 2 * KV_WIDTH
CONV_WIDTH = D_MODEL - ATTN_WIDTH
CONV_KERNEL = 31
CONV_PAD = CONV_KERNEL // 2
FFN_HIDDEN = 2816
FFN_KERNEL = 3
GRID_W = 64
ROPE_THETA = 10000.0
EPS = 1e-6

LANES = 128
SUBLANES = 8
TILE = 256
CONV_HALO = 16
FFN_HALO = 8
FFN_CHUNK = 256
MOD_ROWS = 24
MOD_TN = 1024
VMEM_LIMIT = 56 * 1024 * 1024

NT_DIMS = (((1,), (1,)), ((), ()))


def _params(sem, vmem=VMEM_LIMIT):
    return pltpu.CompilerParams(dimension_semantics=sem, vmem_limit_bytes=vmem)


def _mod_kernel(c_ref, w_ref, b_ref, o_ref):
    c = c_ref[...]
    s = c * jax.nn.sigmoid(c)
    o_ref[0] = jnp.dot(s, w_ref[0], preferred_element_type=F32,
                       precision=lax.Precision.HIGHEST) + b_ref[0]


def _modulation(cc, w_mod, b_mod):
    depth = w_mod.shape[0]
    n = w_mod.shape[2]
    return pl.pallas_call(
        _mod_kernel,
        grid=(depth, n // MOD_TN),
        in_specs=[
            pl.BlockSpec((MOD_ROWS, D_MODEL), lambda l, j: (0, 0)),
            pl.BlockSpec((1, D_MODEL, MOD_TN), lambda l, j: (l, 0, j)),
            pl.BlockSpec((1, 1, MOD_TN), lambda l, j: (l, 0, j)),
        ],
        out_specs=pl.BlockSpec((1, MOD_ROWS, MOD_TN), lambda l, j: (l, 0, j)),
        out_shape=jax.ShapeDtypeStruct((depth, MOD_ROWS, n), F32),
        compiler_params=_params(("parallel", "parallel")),
        name="modulation",
    )(cc, w_mod, b_mod.reshape(depth, 1, n))


def _rms_mod(x, g, scale, shift):
    ms = jnp.mean(x * x, axis=-1, keepdims=True)
    return (x * lax.rsqrt(ms + EPS)) * g * (1.0 + scale) + shift


def _inproj_kernel(x_ref, mod_ref, g_ref, wqkvt_ref, wconv_ref, gq_ref, gk_ref, cos_ref, sin_ref,
                   qt_ref, k_ref, vt_ref, glu_ref):
    h = _rms_mod(x_ref[0], g_ref[...], mod_ref[0, 1:2, :], mod_ref[0, 0:1, :]).astype(BF16)

    u = jnp.dot(h, wconv_ref[...], preferred_element_type=F32)
    glu_ref[0] = u[:, :CONV_WIDTH] * jax.nn.sigmoid(u[:, CONV_WIDTH:])

    qkvt = lax.dot_general(wqkvt_ref[...], h, NT_DIMS, preferred_element_type=F32)
    cos = cos_ref[...]
    sin = sin_ref[...]

    def head(row0, gain, scale):
        blk = qkvt[row0:row0 + HEAD_DIM, :]
        ms = jnp.mean(blk * blk, axis=0, keepdims=True)
        n = blk * lax.rsqrt(ms + EPS) * gain
        swapped = jnp.concatenate([n[16:32], n[0:16], n[48:64], n[32:48]], axis=0)
        return (n * cos + swapped * sin) * scale

    gq = gq_ref[...]
    gk = gk_ref[...]
    qt = jnp.concatenate([head(i * HEAD_DIM, gq, HEAD_DIM ** -0.5) for i in range(N_Q_HEADS)], axis=0)
    qt_ref[0] = qt.astype(BF16)
    kt = jnp.concatenate([head(ATTN_WIDTH + i * HEAD_DIM, gk, 1.0) for i in range(N_KV_HEADS)], axis=0)
    k_ref[0] = kt.T.astype(BF16)
    vt_ref[0] = qkvt[ATTN_WIDTH + KV_WIDTH:, :].astype(BF16)


def _inproj(xs, mods, g, wqkvt, wconv, gq, gk, cos_t, sin_t, n_latent_tiles):
    batch, tokens, _ = xs.shape
    n_tiles = tokens // TILE

    def mod_row(b, t):
        return (jnp.where(t < n_latent_tiles, b, batch), 0, 0)

    const2 = lambda b, t: (0, 0)
    return pl.pallas_call(
        _inproj_kernel,
        grid=(batch, n_tiles),
        in_specs=[
            pl.BlockSpec((1, TILE, D_MODEL), lambda b, t: (b, t, 0)),
            pl.BlockSpec((1, 6, D_MODEL), mod_row),
            pl.BlockSpec((1, D_MODEL), const2),
            pl.BlockSpec((QKV_WIDTH, D_MODEL), const2),
            pl.BlockSpec((D_MODEL, 2 * CONV_WIDTH), const2),
            pl.BlockSpec((HEAD_DIM, TILE), const2),
            pl.BlockSpec((HEAD_DIM, TILE), const2),
            pl.BlockSpec((HEAD_DIM, TILE), lambda b, t: (0, t)),
            pl.BlockSpec((HEAD_DIM, TILE), lambda b, t: (0, t)),
        ],
        out_specs=[
            pl.BlockSpec((1, ATTN_WIDTH, TILE), lambda b, t: (b, 0, t)),
            pl.BlockSpec((1, TILE, KV_WIDTH), lambda b, t: (b, t, 0)),
            pl.BlockSpec((1, KV_WIDTH, TILE), lambda b, t: (b, 0, t)),
            pl.BlockSpec((1, TILE, CONV_WIDTH), lambda b, t: (b, t, 0)),
        ],
        out_shape=[
            jax.ShapeDtypeStruct((batch, ATTN_WIDTH, tokens), BF16),
            jax.ShapeDtypeStruct((batch, tokens, KV_WIDTH), BF16),
            jax.ShapeDtypeStruct((batch, KV_WIDTH, tokens), BF16),
            jax.ShapeDtypeStruct((batch, tokens, CONV_WIDTH), F32),
        ],
        compiler_params=_params(("parallel", "parallel")),
        name="inproj",
    )(xs, mods, g, wqkvt, wconv, gq, gk, cos_t, sin_t)


def _attend(qt_ref, k, vt, o_ref):
    g = pl.program_id(1)

    def query_operand(a):
        qa = qt_ref[0, a * HEAD_DIM:(a + 1) * HEAD_DIM, :]
        zero = jnp.zeros_like(qa)
        return jnp.concatenate([jnp.where(g == 0, qa, zero), jnp.where(g == 1, qa, zero)], axis=0)

    outs = []
    for a in range(0, GROUP, 2):
        w = jnp.concatenate([query_operand(a), query_operand(a + 1)], axis=1)
        st = jnp.dot(k, w, preferred_element_type=F32)
        m = jnp.max(st, axis=0, keepdims=True)
        p = jnp.exp(st - m)
        l = jnp.sum(p, axis=0, keepdims=True)
        ot = jnp.dot(vt, p.astype(BF16), preferred_element_type=F32) / l
        outs += [ot[:, :TILE], ot[:, TILE:]]
    o_ref[0] = jnp.concatenate(outs, axis=0).T.astype(BF16)


def _attn_kernel(qt_ref, k_ref, vt_ref, o_ref, *, n_latent_tiles, n_latent):
    t = pl.program_id(2)

    @pl.when(t < n_latent_tiles)
    def _():
        _attend(qt_ref, k_ref[0], vt_ref[0], o_ref)

    @pl.when(t >= n_latent_tiles)
    def _():
        _attend(qt_ref, k_ref[0, n_latent:, :], vt_ref[0, :, n_latent:], o_ref)


def _attention(qt, k, vt, n_latent_tiles, n_q_tiles):
    batch, tokens, _ = k.shape
    kernel = functools.partial(_attn_kernel, n_latent_tiles=n_latent_tiles,
                               n_latent=n_latent_tiles * TILE)
    return pl.pallas_call(
        kernel,
        grid=(batch, N_KV_HEADS, n_q_tiles),
        in_specs=[
            pl.BlockSpec((1, GROUP * HEAD_DIM, TILE), lambda b, g, t: (b, g, t)),
            pl.BlockSpec((1, tokens, KV_WIDTH), lambda b, g, t: (b, 0, 0)),
            pl.BlockSpec((1, HEAD_DIM, tokens), lambda b, g, t: (b, g, 0)),
        ],
        out_specs=pl.BlockSpec((1, TILE, GROUP * HEAD_DIM), lambda b, g, t: (b, t, g)),
        out_shape=jax.ShapeDtypeStruct((batch, n_q_tiles * TILE, ATTN_WIDTH), BF16),
        compiler_params=_params(("parallel", "parallel", "parallel")),
        name="attention",
    )(qt, k, vt)


def _depthwise31(win_ref, dw_ref):
    half = TILE // 2
    span = half + (CONV_KERNEL // SUBLANES) * SUBLANES
    load = span + SUBLANES
    cols = []
    for c in range(CONV_WIDTH // LANES):
        lanes = slice(c * LANES, (c + 1) * LANES)
        rows = []
        for r0 in range(0, TILE, half):
            acc = jnp.zeros((half, LANES), F32)
            block = win_ref[r0:r0 + load, lanes]
            for r in range(SUBLANES):
                shifted = block if r == 0 else pltpu.roll(block, load - r, 0)
                for a in range(span // SUBLANES - half // SUBLANES + 1):
                    j = a * SUBLANES + r - (CONV_HALO - CONV_PAD)
                    if 0 <= j < CONV_KERNEL:
                        acc = acc + shifted[a * SUBLANES:a * SUBLANES + half] * dw_ref[j:j + 1, lanes]
            rows.append(acc)
        cols.append(jnp.concatenate(rows, axis=0))
    return jnp.concatenate(cols, axis=1)


def _outproj_kernel(glu_ref, glu_prev_ref, glu_next_ref, attn_ref, x_ref, mod_ref, dw_ref, dwb_ref,
                    lng_ref, lnb_ref, wout_ref, o_ref, win_ref, *, n_latent_tiles):
    t = pl.program_id(1)
    has_prev = jnp.logical_and(t != 0, t != n_latent_tiles)
    has_next = jnp.logical_and(t != n_latent_tiles - 1, t != n_latent_tiles)
    win_ref[0:CONV_HALO, :] = jnp.where(has_prev, glu_prev_ref[0], 0.0)
    win_ref[CONV_HALO:CONV_HALO + TILE, :] = glu_ref[0]
    win_ref[CONV_HALO + TILE:, :] = jnp.where(has_next, glu_next_ref[0], 0.0)

    y = _depthwise31(win_ref, dw_ref) + dwb_ref[...]
    mu = jnp.mean(y, axis=-1, keepdims=True)
    d = y - mu
    var = jnp.mean(d * d, axis=-1, keepdims=True)
    z = d * lax.rsqrt(var + EPS) * lng_ref[...] + lnb_ref[...]
    conv = (z * jax.nn.sigmoid(z)).astype(BF16)

    proj = jnp.dot(attn_ref[0], wout_ref[:ATTN_WIDTH, :], preferred_element_type=F32) \
        + jnp.dot(conv, wout_ref[ATTN_WIDTH:, :], preferred_element_type=F32)
    o_ref[0] = x_ref[0] + mod_ref[0, 2:3, :] * proj


def _outproj(glu, attn, xs, mods, dw, dwb, lng, lnb, wout, n_latent_tiles, n_tiles):
    batch = xs.shape[0]
    halo_blocks = glu.shape[1] // CONV_HALO
    per_tile = TILE // CONV_HALO

    def mod_row(b, t):
        return (jnp.where(t < n_latent_tiles, b, batch), 0, 0)

    const2 = lambda b, t: (0, 0)
    kernel = functools.partial(_outproj_kernel, n_latent_tiles=n_latent_tiles)
    return pl.pallas_call(
        kernel,
        grid=(batch, n_tiles),
        in_specs=[
            pl.BlockSpec((1, TILE, CONV_WIDTH), lambda b, t: (b, t, 0)),
            pl.BlockSpec((1, CONV_HALO, CONV_WIDTH),
                         lambda b, t: (b, jnp.maximum(t * per_tile - 1, 0), 0)),
            pl.BlockSpec((1, CONV_HALO, CONV_WIDTH),
                         lambda b, t: (b, jnp.minimum((t + 1) * per_tile, halo_blocks - 1), 0)),
            pl.BlockSpec((1, TILE, ATTN_WIDTH), lambda b, t: (b, t, 0)),
            pl.BlockSpec((1, TILE, D_MODEL), lambda b, t: (b, t, 0)),
            pl.BlockSpec((1, 6, D_MODEL), mod_row),
            pl.BlockSpec((CONV_KERNEL + 1, CONV_WIDTH), const2),
            pl.BlockSpec((1, CONV_WIDTH), const2),
            pl.BlockSpec((1, CONV_WIDTH), const2),
            pl.BlockSpec((1, CONV_WIDTH), const2),
            pl.BlockSpec((D_MODEL, D_MODEL), const2),
        ],
        out_specs=pl.BlockSpec((1, TILE, D_MODEL), lambda b, t: (b, t, 0)),
        out_shape=jax.ShapeDtypeStruct((batch, n_tiles * TILE, D_MODEL), F32),
        scratch_shapes=[pltpu.VMEM((TILE + 2 * CONV_HALO, CONV_WIDTH), F32)],
        compiler_params=_params(("parallel", "parallel")),
        name="outproj",
    )(glu, glu, glu, attn, xs, mods, dw, dwb, lng, lnb, wout)


def _ffn_kernel(x_ref, x_prev_ref, x_next_ref, mod_ref, g_ref, wg_ref, wv_ref, wd_ref, dw_ref, dwb_ref,
                fg_ref, o_ref, xw_ref, gate_ref, act_ref, *, n_latent_tiles, final_norm):
    t = pl.program_id(1)
    has_prev = jnp.logical_and(t != 0, t != n_latent_tiles)
    has_next = jnp.logical_and(t != n_latent_tiles - 1, t != n_latent_tiles)

    xw_ref[0:TILE, :] = x_ref[0]
    xw_ref[TILE:TILE + FFN_HALO, :] = x_next_ref[0]
    xw_ref[TILE + FFN_HALO:, :] = x_prev_ref[0]
    h = _rms_mod(xw_ref[...], g_ref[...], mod_ref[0, 4:5, :], mod_ref[0, 3:4, :]).astype(BF16)
    h_tile = h[0:TILE]

    for c in range(FFN_HIDDEN // FFN_CHUNK):
        cols = slice(c * FFN_CHUNK, (c + 1) * FFN_CHUNK)
        gate = jnp.dot(h, wg_ref[:, cols], preferred_element_type=F32)
        val = jnp.dot(h_tile, wv_ref[:, cols], preferred_element_type=F32)
        gate_ref[0:FFN_HALO, cols] = jnp.where(has_prev, gate[TILE + FFN_HALO:], 0.0)
        gate_ref[FFN_HALO:FFN_HALO + TILE, cols] = gate[0:TILE]
        gate_ref[FFN_HALO + TILE:, cols] = jnp.where(has_next, gate[TILE:TILE + FFN_HALO], 0.0)
        y = dwb_ref[:, cols]
        for j in range(FFN_KERNEL):
            start = FFN_HALO - FFN_KERNEL // 2 + j
            y = y + gate_ref[start:start + TILE, cols] * dw_ref[j:j + 1, cols]
        act_ref[:, cols] = (y * jax.nn.sigmoid(y) * val).astype(BF16)

    down = jnp.dot(act_ref[...], wd_ref[...], preferred_element_type=F32)
    out = x_ref[0] + mod_ref[0, 5:6, :] * down
    if final_norm:
        ms = jnp.mean(out * out, axis=-1, keepdims=True)
        out = out * lax.rsqrt(ms + EPS) * fg_ref[...]
    o_ref[0] = out


def _ffn(x1, mods, g, wg, wv, wd, dw, dwb, fg, n_latent_tiles, final_norm):
    batch, tokens, _ = x1.shape
    n_tiles = tokens // TILE
    halo_blocks = tokens // FFN_HALO
    per_tile = TILE // FFN_HALO

    def mod_row(b, t):
        return (jnp.where(t < n_latent_tiles, b, batch), 0, 0)

    const2 = lambda b, t: (0, 0)
    kernel = functools.partial(_ffn_kernel, n_latent_tiles=n_latent_tiles, final_norm=final_norm)
    return pl.pallas_call(
        kernel,
        grid=(batch, n_tiles),
        in_specs=[
            pl.BlockSpec((1, TILE, D_MODEL), lambda b, t: (b, t, 0)),
            pl.BlockSpec((1, FFN_HALO, D_MODEL),
                         lambda b, t: (b, jnp.maximum(t * per_tile - 1, 0), 0)),
            pl.BlockSpec((1, FFN_HALO, D_MODEL),
                         lambda b, t: (b, jnp.minimum((t + 1) * per_tile, halo_blocks - 1), 0)),
            pl.BlockSpec((1, 6, D_MODEL), mod_row),
            pl.BlockSpec((1, D_MODEL), const2),
            pl.BlockSpec((D_MODEL, FFN_HIDDEN), const2),
            pl.BlockSpec((D_MODEL, FFN_HIDDEN), const2),
            pl.BlockSpec((FFN_HIDDEN, D_MODEL), const2),
            pl.BlockSpec((FFN_KERNEL, FFN_HIDDEN), const2),
            pl.BlockSpec((1, FFN_HIDDEN), const2),
            pl.BlockSpec((1, D_MODEL), const2),
        ],
        out_specs=pl.BlockSpec((1, TILE, D_MODEL), lambda b, t: (b, t, 0)),
        out_shape=jax.ShapeDtypeStruct((batch, tokens, D_MODEL), F32),
        scratch_shapes=[
            pltpu.VMEM((TILE + 2 * FFN_HALO, D_MODEL), F32),
            pltpu.VMEM((TILE + 2 * FFN_HALO, FFN_HIDDEN), F32),
            pltpu.VMEM((TILE, FFN_HIDDEN), BF16),
        ],
        compiler_params=_params(("parallel", "parallel")),
        name="ffn",
    )(x1, x1, x1, mods, g, wg, wv, wd, dw, dwb, fg)


def _rope_tables(n_latent, n_ctx):
    pos = jnp.arange(n_latent, dtype=jnp.int32)
    rc = jnp.stack([pos // GRID_W, pos % GRID_W], axis=0).astype(F32)
    n_freq = HEAD_DIM // 4
    inv_freq = ROPE_THETA ** (-jnp.arange(n_freq, dtype=F32) / n_freq)
    ang = rc[:, None, :] * inv_freq[None, :, None]
    cos, sin = jnp.cos(ang), jnp.sin(ang)
    cos_t = jnp.concatenate([cos, cos], axis=1).reshape(HEAD_DIM, n_latent)
    sin_t = jnp.concatenate([-sin, sin], axis=1).reshape(HEAD_DIM, n_latent)
    cos_t = jnp.concatenate([cos_t, jnp.ones((HEAD_DIM, n_ctx), F32)], axis=1)
    sin_t = jnp.concatenate([sin_t, jnp.zeros((HEAD_DIM, n_ctx), F32)], axis=1)
    return cos_t, sin_t


def kernel(x, c, ctx, c_ctx, w_mod, b_mod, norm1_g, norm2_g, w_in, q_norm_g, k_norm_g,
           conv_dw, conv_dw_b, conv_ln_g, conv_ln_b, w_out, ffn_w_up, ffn_dw, ffn_dw_b,
           ffn_w_down, final_g):
    batch, seq, _ = x.shape
    n_ctx = ctx.shape[1]
    depth = w_mod.shape[0]
    assert seq % TILE == 0 and n_ctx == TILE and batch + 1 <= MOD_ROWS
    n_latent_tiles = seq // TILE
    n_tiles = n_latent_tiles + 1

    cc = jnp.concatenate([c, c_ctx[None, :], jnp.zeros((MOD_ROWS - batch - 1, D_MODEL), F32)], axis=0)
    mods = _modulation(cc, w_mod, b_mod).reshape(depth, MOD_ROWS, 6, D_MODEL)
    cos_t, sin_t = _rope_tables(seq, n_ctx)
    xs = jnp.concatenate([x, ctx], axis=1)

    for l in range(depth):
        last = l == depth - 1
        wqkvt = w_in[l, :, :QKV_WIDTH].T.astype(BF16)
        wconv = w_in[l, :, QKV_WIDTH:].astype(BF16)
        gq = jnp.broadcast_to(q_norm_g[l][:, None], (HEAD_DIM, TILE))
        gk = jnp.broadcast_to(k_norm_g[l][:, None], (HEAD_DIM, TILE))
        qt, k, vt, glu = _inproj(xs, mods[l], norm1_g[l][None, :], wqkvt, wconv, gq, gk, cos_t, sin_t,
                                 n_latent_tiles)
        n_out_tiles = n_latent_tiles if last else n_tiles
        attn = _attention(qt, k, vt, n_latent_tiles, n_out_tiles)
        dw = jnp.concatenate([conv_dw[l], jnp.zeros((1, CONV_WIDTH), F32)], axis=0)
        x1 = _outproj(glu, attn, xs, mods[l], dw, conv_dw_b[l][None, :], conv_ln_g[l][None, :],
                      conv_ln_b[l][None, :], w_out[l].astype(BF16), n_latent_tiles, n_out_tiles)
        xs = _ffn(x1, mods[l], norm2_g[l][None, :], ffn_w_up[l, :, :FFN_HIDDEN].astype(BF16),
                  ffn_w_up[l, :, FFN_HIDDEN:].astype(BF16), ffn_w_down[l].astype(BF16), ffn_dw[l],
                  ffn_dw_b[l][None, :], final_g[None, :], n_latent_tiles, last)
    return xs
```

```python
import functools

import jax
import jax.numpy as jnp
from jax import lax
from jax.experimental import pallas as pl
from jax.experimental.pallas import tpu as pltpu

F32 = jnp.float32
BF16 = jnp.bfloat16

D_MODEL = 1024
HEAD_DIM = 64
N_Q_HEADS = 8
N_KV_HEADS = 2
GROUP = N_Q_HEADS // N_KV_HEADS
ATTN_WIDTH = N_Q_HEADS * HEAD_DIM
KV_WIDTH = N_KV_HEADS * HEAD_DIM
QKV_WIDTH = ATTN_WIDTH + 2 * KV_WIDTH
CONV_WIDTH = D_MODEL - ATTN_WIDTH
CONV_KERNEL = 31
CONV_PAD = CONV_KERNEL // 2
FFN_HIDDEN = 2816
FFN_KERNEL = 3
GRID_W = 64
ROPE_THETA = 10000.0
EPS = 1e-6

LANES = 128
SUBLANES = 8
BF16_ROWS = 16
Q_SCALE = HEAD_DIM ** -0.5 * 1.4426950408889634
TILE = 256
KEY_BLOCK = 256
CONV_HALO = 16
FFN_HALO = 8
FFN_CHUNK = 256
MOD_ROWS = 24
MOD_TN = 1024
VMEM_LIMIT = 56 * 1024 * 1024

NT_DIMS = (((1,), (1,)), ((), ()))


def _params(sem, vmem=VMEM_LIMIT):
    return pltpu.CompilerParams(dimension_semantics=sem, vmem_limit_bytes=vmem)


def _mod_kernel(c_ref, w_ref, b_ref, o_ref):
    c = c_ref[...]
    s = c * jax.nn.sigmoid(c)
    o_ref[0] = jnp.dot(s, w_ref[0], preferred_element_type=F32,
                       precision=lax.Precision.HIGHEST) + b_ref[0]


def _modulation(cc, w_mod, b_mod):
    depth = w_mod.shape[0]
    n = w_mod.shape[2]
    return pl.pallas_call(
        _mod_kernel,
        grid=(depth, n // MOD_TN),
        in_specs=[
            pl.BlockSpec((MOD_ROWS, D_MODEL), lambda l, j: (0, 0)),
            pl.BlockSpec((1, D_MODEL, MOD_TN), lambda l, j: (l, 0, j)),
            pl.BlockSpec((1, 1, MOD_TN), lambda l, j: (l, 0, j)),
        ],
        out_specs=pl.BlockSpec((1, MOD_ROWS, MOD_TN), lambda l, j: (l, 0, j)),
        out_shape=jax.ShapeDtypeStruct((depth, MOD_ROWS, n), F32),
        compiler_params=_params(("parallel", "parallel")),
        name="modulation",
    )(cc, w_mod, b_mod.reshape(depth, 1, n))


def _rms_mod(x, g, scale, shift):
    ms = jnp.mean(x * x, axis=-1, keepdims=True)
    return (x * lax.rsqrt(ms + EPS)) * g * (1.0 + scale) + shift


def _stream_tile(lat_ref, ctx_ref, n_latent_tiles):
    return jnp.where(pl.program_id(1) < n_latent_tiles, lat_ref[0], ctx_ref[0])


def _stream_specs(lat, ctx, n_latent_tiles):
    ctx_block = 0 if ctx.shape[1] == TILE else n_latent_tiles
    return [
        pl.BlockSpec((1, TILE, D_MODEL), lambda b, t: (b, jnp.minimum(t, n_latent_tiles - 1), 0)),
        pl.BlockSpec((1, TILE, D_MODEL), lambda b, t: (b, ctx_block, 0)),
    ]


def _inproj_kernel(xl_ref, xc_ref, mod_ref, g_ref, wqkvt_ref, wconv_ref, gq_ref, gk_ref, cos_ref, sin_ref,
                   qt_ref, k_ref, vt_ref, glu_ref, *, n_latent_tiles):
    x = _stream_tile(xl_ref, xc_ref, n_latent_tiles)
    h = _rms_mod(x, g_ref[...], mod_ref[0, 1:2, :], mod_ref[0, 0:1, :]).astype(BF16)

    u = jnp.dot(h, wconv_ref[...], preferred_element_type=F32)
    glu_ref[0] = u[:, :CONV_WIDTH] * jax.nn.sigmoid(u[:, CONV_WIDTH:])

    qkvt = lax.dot_general(wqkvt_ref[...], h, NT_DIMS, preferred_element_type=F32)
    cos = cos_ref[...]
    sin = sin_ref[...]

    def head(row0, gain, scale):
        blk = qkvt[row0:row0 + HEAD_DIM, :]
        ms = jnp.mean(blk * blk, axis=0, keepdims=True)
        n = blk * lax.rsqrt(ms + EPS) * gain
        swapped = jnp.concatenate([n[16:32], n[0:16], n[48:64], n[32:48]], axis=0)
        return (n * cos + swapped * sin) * scale

    gq = gq_ref[...]
    gk = gk_ref[...]
    qt = jnp.concatenate([head(i * HEAD_DIM, gq, Q_SCALE) for i in range(N_Q_HEADS)], axis=0)
    qt_ref[0] = qt.astype(BF16)
    kt = jnp.concatenate([head(ATTN_WIDTH + i * HEAD_DIM, gk, 1.0) for i in range(N_KV_HEADS)], axis=0)
    k_ref[0] = kt.T.astype(BF16)
    vt_ref[0] = qkvt[ATTN_WIDTH + KV_WIDTH:, :].astype(BF16)


def _inproj(x_lat, x_ctx, mods, g, wqkvt, wconv, gq, gk, cos_t, sin_t, n_latent_tiles):
    batch = x_lat.shape[0]
    n_tiles = n_latent_tiles + 1
    tokens = n_tiles * TILE

    def mod_row(b, t):
        return (jnp.where(t < n_latent_tiles, b, batch), 0, 0)

    const2 = lambda b, t: (0, 0)
    return pl.pallas_call(
        functools.partial(_inproj_kernel, n_latent_tiles=n_latent_tiles),
        grid=(batch, n_tiles),
        in_specs=_stream_specs(x_lat, x_ctx, n_latent_tiles) + [
            pl.BlockSpec((1, 6, D_MODEL), mod_row),
            pl.BlockSpec((1, D_MODEL), const2),
            pl.BlockSpec((QKV_WIDTH, D_MODEL), const2),
            pl.BlockSpec((D_MODEL, 2 * CONV_WIDTH), const2),
            pl.BlockSpec((HEAD_DIM, TILE), const2),
            pl.BlockSpec((HEAD_DIM, TILE), const2),
            pl.BlockSpec((HEAD_DIM, TILE), lambda b, t: (0, t)),
            pl.BlockSpec((HEAD_DIM, TILE), lambda b, t: (0, t)),
        ],
        out_specs=[
            pl.BlockSpec((1, ATTN_WIDTH, TILE), lambda b, t: (b, 0, t)),
            pl.BlockSpec((1, TILE, KV_WIDTH), lambda b, t: (b, t, 0)),
            pl.BlockSpec((1, KV_WIDTH, TILE), lambda b, t: (b, 0, t)),
            pl.BlockSpec((1, TILE, CONV_WIDTH), lambda b, t: (b, t, 0)),
        ],
        out_shape=[
            jax.ShapeDtypeStruct((batch, ATTN_WIDTH, tokens), BF16),
            jax.ShapeDtypeStruct((batch, tokens, KV_WIDTH), BF16),
            jax.ShapeDtypeStruct((batch, KV_WIDTH, tokens), BF16),
            jax.ShapeDtypeStruct((batch, tokens, CONV_WIDTH), F32),
        ],
        compiler_params=_params(("parallel", "parallel")),
        name="inproj",
    )(x_lat, x_ctx, mods, g, wqkvt, wconv, gq, gk, cos_t, sin_t)


def _attend(qt_ref, k_ref, vt_ref, o_ref, key0, n_blocks):
    g = pl.program_id(1)

    def query_operand(a):
        qa = qt_ref[0, a * HEAD_DIM:(a + 1) * HEAD_DIM, :]
        zero = jnp.zeros_like(qa)
        return jnp.concatenate([jnp.where(g == 0, qa, zero), jnp.where(g == 1, qa, zero)], axis=0)

    w = jnp.concatenate([query_operand(a) for a in range(GROUP)], axis=1)

    def keys(i):
        return slice(key0 + i * KEY_BLOCK, key0 + (i + 1) * KEY_BLOCK)

    def scores(i):
        return jnp.dot(k_ref[0, keys(i), :], w, preferred_element_type=F32)

    ones = jnp.ones((BF16_ROWS, KEY_BLOCK), BF16)

    def values(i, p):
        v1 = jnp.concatenate([vt_ref[0, :, keys(i)], ones], axis=0)
        return jnp.dot(v1, p, preferred_element_type=F32)

    s_next = scores(0)
    m = acc = alpha = p_prev = None
    for i in range(n_blocks):
        s = s_next
        if i + 1 < n_blocks:
            s_next = scores(i + 1)
        if i == 1:
            acc = values(0, p_prev)
        elif i > 1:
            acc = alpha * acc + values(i - 1, p_prev)
        if i == 0:
            m = jnp.max(s, axis=0, keepdims=True)
        else:
            m_new = jnp.maximum(m, jnp.max(s, axis=0, keepdims=True))
            alpha = jnp.exp2(m - m_new)
            m = m_new
        p_prev = jnp.exp2(s - m).astype(BF16)
    last = values(n_blocks - 1, p_prev)
    acc = last if n_blocks == 1 else alpha * acc + last
    ot = acc[:HEAD_DIM] / acc[HEAD_DIM:HEAD_DIM + 1]
    heads = jnp.concatenate([ot[:, a * TILE:(a + 1) * TILE] for a in range(GROUP)], axis=0)
    o_ref[0] = heads.T.astype(BF16)


def _attn_kernel(qt_ref, k_ref, vt_ref, o_ref, *, n_latent_tiles, n_tokens):
    t = pl.program_id(2)
    n_latent = n_latent_tiles * TILE

    @pl.when(t < n_latent_tiles)
    def _():
        _attend(qt_ref, k_ref, vt_ref, o_ref, 0, n_tokens // KEY_BLOCK)

    @pl.when(t >= n_latent_tiles)
    def _():
        _attend(qt_ref, k_ref, vt_ref, o_ref, n_latent, (n_tokens - n_latent) // KEY_BLOCK)


def _attention(qt, k, vt, n_latent_tiles, n_q_tiles):
    batch, tokens, _ = k.shape
    kernel = functools.partial(_attn_kernel, n_latent_tiles=n_latent_tiles, n_tokens=tokens)
    return pl.pallas_call(
        kernel,
        grid=(batch, N_KV_HEADS, n_q_tiles),
        in_specs=[
            pl.BlockSpec((1, GROUP * HEAD_DIM, TILE), lambda b, g, t: (b, g, t)),
            pl.BlockSpec((1, tokens, KV_WIDTH), lambda b, g, t: (b, 0, 0)),
            pl.BlockSpec((1, HEAD_DIM, tokens), lambda b, g, t: (b, g, 0)),
        ],
        out_specs=pl.BlockSpec((1, TILE, GROUP * HEAD_DIM), lambda b, g, t: (b, t, g)),
        out_shape=jax.ShapeDtypeStruct((batch, n_q_tiles * TILE, ATTN_WIDTH), BF16),
        compiler_params=_params(("parallel", "parallel", "parallel")),
        name="attention",
    )(qt, k, vt)


def _depthwise31(win_ref, dw_ref):
    half = TILE // 2
    span = half + (CONV_KERNEL // SUBLANES) * SUBLANES
    load = span + SUBLANES
    cols = []
    for c in range(CONV_WIDTH // LANES):
        lanes = slice(c * LANES, (c + 1) * LANES)
        rows = []
        for r0 in range(0, TILE, half):
            acc = jnp.zeros((half, LANES), F32)
            block = win_ref[r0:r0 + load, lanes]
            for r in range(SUBLANES):
                shifted = block if r == 0 else pltpu.roll(block, load - r, 0)
                for a in range(span // SUBLANES - half // SUBLANES + 1):
                    j = a * SUBLANES + r - (CONV_HALO - CONV_PAD)
                    if 0 <= j < CONV_KERNEL:
                        acc = acc + shifted[a * SUBLANES:a * SUBLANES + half] * dw_ref[j:j + 1, lanes]
            rows.append(acc)
        cols.append(jnp.concatenate(rows, axis=0))
    return jnp.concatenate(cols, axis=1)


def _outproj_kernel(glu_ref, glu_prev_ref, glu_next_ref, attn_ref, xl_ref, xc_ref, mod_ref, dw_ref, dwb_ref,
                    lng_ref, lnb_ref, wout_ref, o_ref, win_ref, *, n_latent_tiles):
    t = pl.program_id(1)
    has_prev = jnp.logical_and(t != 0, t != n_latent_tiles)
    has_next = jnp.logical_and(t != n_latent_tiles - 1, t != n_latent_tiles)
    win_ref[0:CONV_HALO, :] = jnp.where(has_prev, glu_prev_ref[0], 0.0)
    win_ref[CONV_HALO:CONV_HALO + TILE, :] = glu_ref[0]
    win_ref[CONV_HALO + TILE:, :] = jnp.where(has_next, glu_next_ref[0], 0.0)

    y = _depthwise31(win_ref, dw_ref) + dwb_ref[...]
    mu = jnp.mean(y, axis=-1, keepdims=True)
    d = y - mu
    var = jnp.mean(d * d, axis=-1, keepdims=True)
    z = d * lax.rsqrt(var + EPS) * lng_ref[...] + lnb_ref[...]
    conv = (z * jax.nn.sigmoid(z)).astype(BF16)

    proj = jnp.dot(attn_ref[0], wout_ref[:ATTN_WIDTH, :], preferred_element_type=F32) \
        + jnp.dot(conv, wout_ref[ATTN_WIDTH:, :], preferred_element_type=F32)
    o_ref[0] = _stream_tile(xl_ref, xc_ref, n_latent_tiles) + mod_ref[0, 2:3, :] * proj


def _outproj(glu, attn, x_lat, x_ctx, mods, dw, dwb, lng, lnb, wout, n_latent_tiles, n_tiles):
    batch = x_lat.shape[0]
    halo_blocks = glu.shape[1] // CONV_HALO
    per_tile = TILE // CONV_HALO

    def mod_row(b, t):
        return (jnp.where(t < n_latent_tiles, b, batch), 0, 0)

    const2 = lambda b, t: (0, 0)
    kernel = functools.partial(_outproj_kernel, n_latent_tiles=n_latent_tiles)
    return pl.pallas_call(
        kernel,
        grid=(batch, n_tiles),
        in_specs=[
            pl.BlockSpec((1, TILE, CONV_WIDTH), lambda b, t: (b, t, 0)),
            pl.BlockSpec((1, CONV_HALO, CONV_WIDTH),
                         lambda b, t: (b, jnp.maximum(t * per_tile - 1, 0), 0)),
            pl.BlockSpec((1, CONV_HALO, CONV_WIDTH),
                         lambda b, t: (b, jnp.minimum((t + 1) * per_tile, halo_blocks - 1), 0)),
            pl.BlockSpec((1, TILE, ATTN_WIDTH), lambda b, t: (b, t, 0)),
        ] + _stream_specs(x_lat, x_ctx, n_latent_tiles) + [
            pl.BlockSpec((1, 6, D_MODEL), mod_row),
            pl.BlockSpec((CONV_KERNEL + 1, CONV_WIDTH), const2),
            pl.BlockSpec((1, CONV_WIDTH), const2),
            pl.BlockSpec((1, CONV_WIDTH), const2),
            pl.BlockSpec((1, CONV_WIDTH), const2),
            pl.BlockSpec((D_MODEL, D_MODEL), const2),
        ],
        out_specs=pl.BlockSpec((1, TILE, D_MODEL), lambda b, t: (b, t, 0)),
        out_shape=jax.ShapeDtypeStruct((batch, n_tiles * TILE, D_MODEL), F32),
        scratch_shapes=[pltpu.VMEM((TILE + 2 * CONV_HALO, CONV_WIDTH), F32)],
        compiler_params=_params(("parallel", "parallel")),
        name="outproj",
    )(glu, glu, glu, attn, x_lat, x_ctx, mods, dw, dwb, lng, lnb, wout)


def _ffn_kernel(x_ref, x_prev_ref, x_next_ref, mod_ref, g_ref, wg_ref, wv_ref, wd_ref, dw_ref, dwb_ref,
                fg_ref, o_ref, xw_ref, gate_ref, act_ref, *, n_latent_tiles, final_norm):
    t = pl.program_id(1)
    has_prev = jnp.logical_and(t != 0, t != n_latent_tiles)
    has_next = jnp.logical_and(t != n_latent_tiles - 1, t != n_latent_tiles)

    xw_ref[0:TILE, :] = x_ref[0]
    xw_ref[TILE:TILE + FFN_HALO, :] = x_next_ref[0]
    xw_ref[TILE + FFN_HALO:, :] = x_prev_ref[0]
    h = _rms_mod(xw_ref[...], g_ref[...], mod_ref[0, 4:5, :], mod_ref[0, 3:4, :]).astype(BF16)
    h_tile = h[0:TILE]

    for c in range(FFN_HIDDEN // FFN_CHUNK):
        cols = slice(c * FFN_CHUNK, (c + 1) * FFN_CHUNK)
        gate = jnp.dot(h, wg_ref[:, cols], preferred_element_type=F32)
        val = jnp.dot(h_tile, wv_ref[:, cols], preferred_element_type=F32)
        gate_ref[0:FFN_HALO, cols] = jnp.where(has_prev, gate[TILE + FFN_HALO:], 0.0)
        gate_ref[FFN_HALO:FFN_HALO + TILE, cols] = gate[0:TILE]
        gate_ref[FFN_HALO + TILE:, cols] = jnp.where(has_next, gate[TILE:TILE + FFN_HALO], 0.0)
        y = dwb_ref[:, cols]
        for j in range(FFN_KERNEL):
            start = FFN_HALO - FFN_KERNEL // 2 + j
            y = y + gate_ref[start:start + TILE, cols] * dw_ref[j:j + 1, cols]
        act_ref[:, cols] = (y * jax.nn.sigmoid(y) * val).astype(BF16)

    down = jnp.dot(act_ref[...], wd_ref[...], preferred_element_type=F32)
    out = x_ref[0] + mod_ref[0, 5:6, :] * down
    if final_norm:
        ms = jnp.mean(out * out, axis=-1, keepdims=True)
        out = out * lax.rsqrt(ms + EPS) * fg_ref[...]
    o_ref[0] = out


def _ffn(x1, mods, g, wg, wv, wd, dw, dwb, fg, n_latent_tiles, final_norm):
    batch, tokens, _ = x1.shape
    n_tiles = tokens // TILE
    halo_blocks = tokens // FFN_HALO
    per_tile = TILE // FFN_HALO

    def mod_row(b, t):
        return (jnp.where(t < n_latent_tiles, b, batch), 0, 0)

    const2 = lambda b, t: (0, 0)
    kernel = functools.partial(_ffn_kernel, n_latent_tiles=n_latent_tiles, final_norm=final_norm)
    return pl.pallas_call(
        kernel,
        grid=(batch, n_tiles),
        in_specs=[
            pl.BlockSpec((1, TILE, D_MODEL), lambda b, t: (b, t, 0)),
            pl.BlockSpec((1, FFN_HALO, D_MODEL),
                         lambda b, t: (b, jnp.maximum(t * per_tile - 1, 0), 0)),
            pl.BlockSpec((1, FFN_HALO, D_MODEL),
                         lambda b, t: (b, jnp.minimum((t + 1) * per_tile, halo_blocks - 1), 0)),
            pl.BlockSpec((1, 6, D_MODEL), mod_row),
            pl.BlockSpec((1, D_MODEL), const2),
            pl.BlockSpec((D_MODEL, FFN_HIDDEN), const2),
            pl.BlockSpec((D_MODEL, FFN_HIDDEN), const2),
            pl.BlockSpec((FFN_HIDDEN, D_MODEL), const2),
            pl.BlockSpec((FFN_KERNEL, FFN_HIDDEN), const2),
            pl.BlockSpec((1, FFN_HIDDEN), const2),
            pl.BlockSpec((1, D_MODEL), const2),
        ],
        out_specs=pl.BlockSpec((1, TILE, D_MODEL), lambda b, t: (b, t, 0)),
        out_shape=jax.ShapeDtypeStruct((batch, tokens, D_MODEL), F32),
        scratch_shapes=[
            pltpu.VMEM((TILE + 2 * FFN_HALO, D_MODEL), F32),
            pltpu.VMEM((TILE + 2 * FFN_HALO, FFN_HIDDEN), F32),
            pltpu.VMEM((TILE, FFN_HIDDEN), BF16),
        ],
        compiler_params=_params(("parallel", "parallel")),
        name="ffn",
    )(x1, x1, x1, mods, g, wg, wv, wd, dw, dwb, fg)


def _rope_tables(n_latent, n_ctx):
    pos = jnp.arange(n_latent, dtype=jnp.int32)
    rc = jnp.stack([pos // GRID_W, pos % GRID_W], axis=0).astype(F32)
    n_freq = HEAD_DIM // 4
    inv_freq = ROPE_THETA ** (-jnp.arange(n_freq, dtype=F32) / n_freq)
    ang = rc[:, None, :] * inv_freq[None, :, None]
    cos, sin = jnp.cos(ang), jnp.sin(ang)
    cos_t = jnp.concatenate([cos, cos], axis=1).reshape(HEAD_DIM, n_latent)
    sin_t = jnp.concatenate([-sin, sin], axis=1).reshape(HEAD_DIM, n_latent)
    cos_t = jnp.concatenate([cos_t, jnp.ones((HEAD_DIM, n_ctx), F32)], axis=1)
    sin_t = jnp.concatenate([sin_t, jnp.zeros((HEAD_DIM, n_ctx), F32)], axis=1)
    return cos_t, sin_t


def kernel(x, c, ctx, c_ctx, w_mod, b_mod, norm1_g, norm2_g, w_in, q_norm_g, k_norm_g,
           conv_dw, conv_dw_b, conv_ln_g, conv_ln_b, w_out, ffn_w_up, ffn_dw, ffn_dw_b,
           ffn_w_down, final_g):
    batch, seq, _ = x.shape
    n_ctx = ctx.shape[1]
    depth = w_mod.shape[0]
    assert seq % TILE == 0 and n_ctx == TILE and batch + 1 <= MOD_ROWS
    n_latent_tiles = seq // TILE
    n_tiles = n_latent_tiles + 1

    cc = jnp.concatenate([c, c_ctx[None, :], jnp.zeros((MOD_ROWS - batch - 1, D_MODEL), F32)], axis=0)
    mods = _modulation(cc, w_mod, b_mod).reshape(depth, MOD_ROWS, 6, D_MODEL)
    cos_t, sin_t = _rope_tables(seq, n_ctx)
    x_lat, x_ctx = x, ctx

    for l in range(depth):
        last = l == depth - 1
        wqkvt = w_in[l, :, :QKV_WIDTH].T.astype(BF16)
        wconv = w_in[l, :, QKV_WIDTH:].astype(BF16)
        gq = jnp.broadcast_to(q_norm_g[l][:, None], (HEAD_DIM, TILE))
        gk = jnp.broadcast_to(k_norm_g[l][:, None], (HEAD_DIM, TILE))
        qt, k, vt, glu = _inproj(x_lat, x_ctx, mods[l], norm1_g[l][None, :], wqkvt, wconv, gq, gk,
                                 cos_t, sin_t, n_latent_tiles)
        n_out_tiles = n_latent_tiles if last else n_tiles
        attn = _attention(qt, k, vt, n_latent_tiles, n_out_tiles)
        dw = jnp.concatenate([conv_dw[l], jnp.zeros((1, CONV_WIDTH), F32)], axis=0)
        x1 = _outproj(glu, attn, x_lat, x_ctx, mods[l], dw, conv_dw_b[l][None, :], conv_ln_g[l][None, :],
                      conv_ln_b[l][None, :], w_out[l].astype(BF16), n_latent_tiles, n_out_tiles)
        x_lat = x_ctx = _ffn(x1, mods[l], norm2_g[l][None, :], ffn_w_up[l, :, :FFN_HIDDEN].astype(BF16),
                             ffn_w_up[l, :, FFN_HIDDEN:].astype(BF16), ffn_w_down[l].astype(BF16),
                             ffn_dw[l], ffn_dw_b[l][None, :], final_g[None, :], n_latent_tiles, last)
    return x_lat
```

```python
import functools

import jax
import jax.numpy as jnp
from jax import lax
from jax.experimental import pallas as pl
from jax.experimental.pallas import tpu as pltpu

F32 = jnp.float32
BF16 = jnp.bfloat16

D_MODEL = 1024
HEAD_DIM = 64
N_Q_HEADS = 8
N_KV_HEADS = 2
GROUP = N_Q_HEADS // N_KV_HEADS
ATTN_WIDTH = N_Q_HEADS * HEAD_DIM
KV_WIDTH = N_KV_HEADS * HEAD_DIM
QKV_WIDTH = ATTN_WIDTH + 2 * KV_WIDTH
CONV_WIDTH = D_MODEL - ATTN_WIDTH
CONV_KERNEL = 31
CONV_PAD = CONV_KERNEL // 2
FFN_HIDDEN = 2816
FFN_KERNEL = 3
GRID_W = 64
ROPE_THETA = 10000.0
EPS = 1e-6

LANES = 128
SUBLANES = 8
BF16_ROWS = 16
Q_SCALE = HEAD_DIM ** -0.5 * 1.4426950408889634
TILE = 256
KEY_BLOCK = 256
PAIR = 2
CONV_HALO = 16
FFN_HALO = 8
FFN_CHUNK = 256
MOD_ROWS = 24
MOD_TN = 1024
VMEM_LIMIT = 56 * 1024 * 1024

NT_DIMS = (((1,), (1,)), ((), ()))


def _params(sem, vmem=VMEM_LIMIT):
    return pltpu.CompilerParams(dimension_semantics=sem, vmem_limit_bytes=vmem)


def _mod_kernel(c_ref, w_ref, b_ref, o_ref):
    c = c_ref[...]
    s = c * jax.nn.sigmoid(c)
    o_ref[0] = jnp.dot(s, w_ref[0], preferred_element_type=F32,
                       precision=lax.Precision.HIGHEST) + b_ref[0]


def _modulation(cc, w_mod, b_mod):
    depth = w_mod.shape[0]
    n = w_mod.shape[2]
    return pl.pallas_call(
        _mod_kernel,
        grid=(depth, n // MOD_TN),
        in_specs=[
            pl.BlockSpec((MOD_ROWS, D_MODEL), lambda l, j: (0, 0)),
            pl.BlockSpec((1, D_MODEL, MOD_TN), lambda l, j: (l, 0, j)),
            pl.BlockSpec((1, 1, MOD_TN), lambda l, j: (l, 0, j)),
        ],
        out_specs=pl.BlockSpec((1, MOD_ROWS, MOD_TN), lambda l, j: (l, 0, j)),
        out_shape=jax.ShapeDtypeStruct((depth, MOD_ROWS, n), F32),
        compiler_params=_params(("parallel", "parallel")),
        name="modulation",
    )(cc, w_mod, b_mod.reshape(depth, 1, n))


def _rms_mod(x, g, scale, shift):
    ms = jnp.mean(x * x, axis=-1, keepdims=True)
    return (x * lax.rsqrt(ms + EPS)) * g * (1.0 + scale) + shift


def _stream_tile(lat_ref, ctx_ref, n_latent_tiles, e):
    return jnp.where(pl.program_id(1) < n_latent_tiles, lat_ref[e], ctx_ref[e])


def _stream_specs(lat, ctx, n_latent_tiles):
    ctx_block = 0 if ctx.shape[1] == TILE else n_latent_tiles
    return [
        pl.BlockSpec((PAIR, TILE, D_MODEL), lambda b, t: (b, jnp.minimum(t, n_latent_tiles - 1), 0)),
        pl.BlockSpec((PAIR, TILE, D_MODEL), lambda b, t: (b, ctx_block, 0)),
    ]


def _mod_pair(n_latent_tiles, batch):
    return lambda b, t: (jnp.where(t < n_latent_tiles, b, batch // PAIR), 0, 0)


def _resident(shape):
    return pl.BlockSpec(shape, lambda b, t: (0,) * len(shape), pipeline_mode=pl.Buffered(1))


def _inproj_kernel(xl_ref, xc_ref, mod_ref, g_ref, wqkvt_ref, wconv_ref, gq_ref, gk_ref, cos_ref, sin_ref,
                   qt_ref, k_ref, vt_ref, glu_ref, *, n_latent_tiles):
    h = jnp.concatenate(
        [_rms_mod(_stream_tile(xl_ref, xc_ref, n_latent_tiles, e), g_ref[...],
                  mod_ref[e, 1:2, :], mod_ref[e, 0:1, :]).astype(BF16) for e in range(PAIR)], axis=0)

    u = jnp.dot(h, wconv_ref[...], preferred_element_type=F32)
    glu = u[:, :CONV_WIDTH] * jax.nn.sigmoid(u[:, CONV_WIDTH:])
    for e in range(PAIR):
        glu_ref[e] = glu[e * TILE:(e + 1) * TILE]

    qkvt = lax.dot_general(wqkvt_ref[...], h, NT_DIMS, preferred_element_type=F32)
    cos = jnp.concatenate([cos_ref[...]] * PAIR, axis=1)
    sin = jnp.concatenate([sin_ref[...]] * PAIR, axis=1)

    def head(row0, gain, scale):
        blk = qkvt[row0:row0 + HEAD_DIM, :]
        ms = jnp.mean(blk * blk, axis=0, keepdims=True)
        n = blk * lax.rsqrt(ms + EPS) * gain
        swapped = jnp.concatenate([n[16:32], n[0:16], n[48:64], n[32:48]], axis=0)
        return (n * cos + swapped * sin) * scale

    gq = gq_ref[...]
    gk = gk_ref[...]
    qt = jnp.concatenate([head(i * HEAD_DIM, gq, Q_SCALE) for i in range(N_Q_HEADS)], axis=0)
    kt = jnp.concatenate([head(ATTN_WIDTH + i * HEAD_DIM, gk, 1.0) for i in range(N_KV_HEADS)], axis=0)
    for e in range(PAIR):
        cols = slice(e * TILE, (e + 1) * TILE)
        qt_ref[e] = qt[:, cols].astype(BF16)
        k_ref[e] = kt[:, cols].T.astype(BF16)
        vt_ref[e] = qkvt[ATTN_WIDTH + KV_WIDTH:, cols].astype(BF16)


def _inproj(x_lat, x_ctx, mods, g, wqkvt, wconv, gq, gk, cos_t, sin_t, n_latent_tiles):
    batch = x_lat.shape[0]
    n_tiles = n_latent_tiles + 1
    tokens = n_tiles * TILE
    return pl.pallas_call(
        functools.partial(_inproj_kernel, n_latent_tiles=n_latent_tiles),
        grid=(batch // PAIR, n_tiles),
        in_specs=_stream_specs(x_lat, x_ctx, n_latent_tiles) + [
            pl.BlockSpec((PAIR, 6, D_MODEL), _mod_pair(n_latent_tiles, batch)),
            _resident((1, D_MODEL)),
            _resident((QKV_WIDTH, D_MODEL)),
            _resident((D_MODEL, 2 * CONV_WIDTH)),
            _resident((HEAD_DIM, PAIR * TILE)),
            _resident((HEAD_DIM, PAIR * TILE)),
            pl.BlockSpec((HEAD_DIM, TILE), lambda b, t: (0, t)),
            pl.BlockSpec((HEAD_DIM, TILE), lambda b, t: (0, t)),
        ],
        out_specs=[
            pl.BlockSpec((PAIR, ATTN_WIDTH, TILE), lambda b, t: (b, 0, t)),
            pl.BlockSpec((PAIR, TILE, KV_WIDTH), lambda b, t: (b, t, 0)),
            pl.BlockSpec((PAIR, KV_WIDTH, TILE), lambda b, t: (b, 0, t)),
            pl.BlockSpec((PAIR, TILE, CONV_WIDTH), lambda b, t: (b, t, 0)),
        ],
        out_shape=[
            jax.ShapeDtypeStruct((batch, ATTN_WIDTH, tokens), BF16),
            jax.ShapeDtypeStruct((batch, tokens, KV_WIDTH), BF16),
            jax.ShapeDtypeStruct((batch, KV_WIDTH, tokens), BF16),
            jax.ShapeDtypeStruct((batch, tokens, CONV_WIDTH), F32),
        ],
        compiler_params=_params(("parallel", "parallel")),
        name="inproj",
    )(x_lat, x_ctx, mods, g, wqkvt, wconv, gq, gk, cos_t, sin_t)


def _attend(qt_ref, k_ref, vt_ref, o_ref, key0, n_blocks):
    def query_operand(g):
        zero = jnp.zeros((HEAD_DIM, TILE), BF16)
        cols = []
        for a in range(GROUP):
            qa = qt_ref[0, (g * GROUP + a) * HEAD_DIM:(g * GROUP + a + 1) * HEAD_DIM, :]
            cols.append(jnp.concatenate([qa if j == g else zero for j in range(N_KV_HEADS)], axis=0))
        return jnp.concatenate(cols, axis=1)

    w = [query_operand(g) for g in range(N_KV_HEADS)]

    def keys(i):
        return slice(key0 + i * KEY_BLOCK, key0 + (i + 1) * KEY_BLOCK)

    def scores(g, i):
        return jnp.dot(k_ref[0, keys(i), :], w[g], preferred_element_type=F32)

    ones = jnp.ones((BF16_ROWS, KEY_BLOCK), BF16)

    def values(g, i, p):
        v1 = jnp.concatenate([vt_ref[0, g * HEAD_DIM:(g + 1) * HEAD_DIM, keys(i)], ones], axis=0)
        return jnp.dot(v1, p, preferred_element_type=F32)

    items = [(g, i) for g in range(N_KV_HEADS) for i in range(n_blocks)]
    m = [None] * N_KV_HEADS
    acc = [None] * N_KV_HEADS

    def fold(g, i, p, alpha):
        v = values(g, i, p)
        acc[g] = v if i == 0 else alpha * acc[g] + v

    s_next = scores(*items[0])
    pending = None
    for n, (g, i) in enumerate(items):
        s = s_next
        if n + 1 < len(items):
            s_next = scores(*items[n + 1])
        if pending is not None:
            fold(*pending)
        alpha = None
        if i == 0:
            m[g] = jnp.max(s, axis=0, keepdims=True)
        else:
            m_new = jnp.maximum(m[g], jnp.max(s, axis=0, keepdims=True))
            alpha = jnp.exp2(m[g] - m_new)
            m[g] = m_new
        pending = (g, i, jnp.exp2(s - m[g]).astype(BF16), alpha)
    fold(*pending)

    heads = []
    for g in range(N_KV_HEADS):
        ot = acc[g][:HEAD_DIM] / acc[g][HEAD_DIM:HEAD_DIM + 1]
        heads += [ot[:, a * TILE:(a + 1) * TILE] for a in range(GROUP)]
    o_ref[0] = jnp.concatenate(heads, axis=0).T.astype(BF16)


def _attn_kernel(qt_ref, k_ref, vt_ref, o_ref, *, n_latent_tiles, n_tokens):
    t = pl.program_id(1)
    n_latent = n_latent_tiles * TILE

    @pl.when(t < n_latent_tiles)
    def _():
        _attend(qt_ref, k_ref, vt_ref, o_ref, 0, n_tokens // KEY_BLOCK)

    @pl.when(t >= n_latent_tiles)
    def _():
        _attend(qt_ref, k_ref, vt_ref, o_ref, n_latent, (n_tokens - n_latent) // KEY_BLOCK)


def _attention(qt, k, vt, n_latent_tiles, n_q_tiles):
    batch, tokens, _ = k.shape
    kernel = functools.partial(_attn_kernel, n_latent_tiles=n_latent_tiles, n_tokens=tokens)
    return pl.pallas_call(
        kernel,
        grid=(batch, n_q_tiles),
        in_specs=[
            pl.BlockSpec((1, ATTN_WIDTH, TILE), lambda b, t: (b, 0, t)),
            pl.BlockSpec((1, tokens, KV_WIDTH), lambda b, t: (b, 0, 0)),
            pl.BlockSpec((1, KV_WIDTH, tokens), lambda b, t: (b, 0, 0)),
        ],
        out_specs=pl.BlockSpec((1, TILE, ATTN_WIDTH), lambda b, t: (b, t, 0)),
        out_shape=jax.ShapeDtypeStruct((batch, n_q_tiles * TILE, ATTN_WIDTH), BF16),
        compiler_params=_params(("parallel", "parallel")),
        name="attention",
    )(qt, k, vt)


def _depthwise31(win_ref, dw_ref):
    half = TILE // 2
    span = half + (CONV_KERNEL // SUBLANES) * SUBLANES
    load = span + SUBLANES
    cols = []
    for c in range(CONV_WIDTH // LANES):
        lanes = slice(c * LANES, (c + 1) * LANES)
        rows = []
        for r0 in range(0, TILE, half):
            acc = jnp.zeros((half, LANES), F32)
            block = win_ref[r0:r0 + load, lanes]
            for r in range(SUBLANES):
                shifted = block if r == 0 else pltpu.roll(block, load - r, 0)
                for a in range(span // SUBLANES - half // SUBLANES + 1):
                    j = a * SUBLANES + r - (CONV_HALO - CONV_PAD)
                    if 0 <= j < CONV_KERNEL:
                        acc = acc + shifted[a * SUBLANES:a * SUBLANES + half] * dw_ref[j:j + 1, lanes]
            rows.append(acc)
        cols.append(jnp.concatenate(rows, axis=0))
    return jnp.concatenate(cols, axis=1)


def _outproj_kernel(glu_ref, glu_prev_ref, glu_next_ref, attn_ref, xl_ref, xc_ref, mod_ref, dw_ref, dwb_ref,
                    lng_ref, lnb_ref, wout_ref, o_ref, win_ref, *, n_latent_tiles):
    t = pl.program_id(1)
    has_prev = jnp.logical_and(t != 0, t != n_latent_tiles)
    has_next = jnp.logical_and(t != n_latent_tiles - 1, t != n_latent_tiles)
    ys = []
    for e in range(PAIR):
        win = win_ref.at[e]
        win[0:CONV_HALO, :] = jnp.where(has_prev, glu_prev_ref[e], 0.0)
        win[CONV_HALO:CONV_HALO + TILE, :] = glu_ref[e]
        win[CONV_HALO + TILE:, :] = jnp.where(has_next, glu_next_ref[e], 0.0)
        ys.append(_depthwise31(win, dw_ref))
    y = jnp.concatenate(ys, axis=0) + dwb_ref[...]

    mu = jnp.mean(y, axis=-1, keepdims=True)
    d = y - mu
    var = jnp.mean(d * d, axis=-1, keepdims=True)
    z = d * lax.rsqrt(var + EPS) * lng_ref[...] + lnb_ref[...]
    conv = (z * jax.nn.sigmoid(z)).astype(BF16)

    attn = jnp.concatenate([attn_ref[e] for e in range(PAIR)], axis=0)
    proj = jnp.dot(attn, wout_ref[:ATTN_WIDTH, :], preferred_element_type=F32) \
        + jnp.dot(conv, wout_ref[ATTN_WIDTH:, :], preferred_element_type=F32)
    for e in range(PAIR):
        o_ref[e] = _stream_tile(xl_ref, xc_ref, n_latent_tiles, e) \
            + mod_ref[e, 2:3, :] * proj[e * TILE:(e + 1) * TILE]


def _outproj(glu, attn, x_lat, x_ctx, mods, dw, dwb, lng, lnb, wout, n_latent_tiles, n_tiles):
    batch = x_lat.shape[0]
    halo_blocks = glu.shape[1] // CONV_HALO
    per_tile = TILE // CONV_HALO
    kernel = functools.partial(_outproj_kernel, n_latent_tiles=n_latent_tiles)
    return pl.pallas_call(
        kernel,
        grid=(batch // PAIR, n_tiles),
        in_specs=[
            pl.BlockSpec((PAIR, TILE, CONV_WIDTH), lambda b, t: (b, t, 0)),
            pl.BlockSpec((PAIR, CONV_HALO, CONV_WIDTH),
                         lambda b, t: (b, jnp.maximum(t * per_tile - 1, 0), 0)),
            pl.BlockSpec((PAIR, CONV_HALO, CONV_WIDTH),
                         lambda b, t: (b, jnp.minimum((t + 1) * per_tile, halo_blocks - 1), 0)),
            pl.BlockSpec((PAIR, TILE, ATTN_WIDTH), lambda b, t: (b, t, 0)),
        ] + _stream_specs(x_lat, x_ctx, n_latent_tiles) + [
            pl.BlockSpec((PAIR, 6, D_MODEL), _mod_pair(n_latent_tiles, batch)),
            _resident((CONV_KERNEL + 1, CONV_WIDTH)),
            _resident((1, CONV_WIDTH)),
            _resident((1, CONV_WIDTH)),
            _resident((1, CONV_WIDTH)),
            _resident((D_MODEL, D_MODEL)),
        ],
        out_specs=pl.BlockSpec((PAIR, TILE, D_MODEL), lambda b, t: (b, t, 0)),
        out_shape=jax.ShapeDtypeStruct((batch, n_tiles * TILE, D_MODEL), F32),
        scratch_shapes=[pltpu.VMEM((PAIR, TILE + 2 * CONV_HALO, CONV_WIDTH), F32)],
        compiler_params=_params(("parallel", "parallel")),
        name="outproj",
    )(glu, glu, glu, attn, x_lat, x_ctx, mods, dw, dwb, lng, lnb, wout)


def _ffn_kernel(x_ref, x_prev_ref, x_next_ref, mod_ref, g_ref, wg_ref, wv_ref, wd_ref, dw_ref, dwb_ref,
                fg_ref, o_ref, xw_ref, gate_ref, act_ref, *, n_latent_tiles, final_norm):
    t = pl.program_id(1)
    has_prev = jnp.logical_and(t != 0, t != n_latent_tiles)
    has_next = jnp.logical_and(t != n_latent_tiles - 1, t != n_latent_tiles)
    halo0 = PAIR * TILE

    tiles, halos = [], []
    for e in range(PAIR):
        xw_ref[e, 0:TILE, :] = x_ref[e]
        xw_ref[e, TILE:TILE + FFN_HALO, :] = x_next_ref[e]
        xw_ref[e, TILE + FFN_HALO:, :] = x_prev_ref[e]
        he = _rms_mod(xw_ref[e], g_ref[...], mod_ref[e, 4:5, :], mod_ref[e, 3:4, :]).astype(BF16)
        tiles.append(he[0:TILE])
        halos.append(he[TILE:])
    h = jnp.concatenate(tiles + halos, axis=0)
    h_tiles = h[0:halo0]

    for c in range(FFN_HIDDEN // FFN_CHUNK):
        cols = slice(c * FFN_CHUNK, (c + 1) * FFN_CHUNK)
        gate = jnp.dot(h, wg_ref[:, cols], preferred_element_type=F32)
        val = jnp.dot(h_tiles, wv_ref[:, cols], preferred_element_type=F32)
        for e in range(PAIR):
            nxt = halo0 + 2 * FFN_HALO * e
            gate_ref[e, 0:FFN_HALO, cols] = jnp.where(has_prev, gate[nxt + FFN_HALO:nxt + 2 * FFN_HALO], 0.0)
            gate_ref[e, FFN_HALO:FFN_HALO + TILE, cols] = gate[e * TILE:(e + 1) * TILE]
            gate_ref[e, FFN_HALO + TILE:, cols] = jnp.where(has_next, gate[nxt:nxt + FFN_HALO], 0.0)
            y = dwb_ref[:, cols]
            for j in range(FFN_KERNEL):
                start = FFN_HALO - FFN_KERNEL // 2 + j
                y = y + gate_ref[e, start:start + TILE, cols] * dw_ref[j:j + 1, cols]
            act_ref[e * TILE:(e + 1) * TILE, cols] = \
                (y * jax.nn.sigmoid(y) * val[e * TILE:(e + 1) * TILE]).astype(BF16)

    down = jnp.dot(act_ref[...], wd_ref[...], preferred_element_type=F32)
    for e in range(PAIR):
        out = x_ref[e] + mod_ref[e, 5:6, :] * down[e * TILE:(e + 1) * TILE]
        if final_norm:
            ms = jnp.mean(out * out, axis=-1, keepdims=True)
            out = out * lax.rsqrt(ms + EPS) * fg_ref[...]
        o_ref[e] = out


def _ffn(x1, mods, g, wg, wv, wd, dw, dwb, fg, n_latent_tiles, final_norm):
    batch, tokens, _ = x1.shape
    n_tiles = tokens // TILE
    halo_blocks = tokens // FFN_HALO
    per_tile = TILE // FFN_HALO
    kernel = functools.partial(_ffn_kernel, n_latent_tiles=n_latent_tiles, final_norm=final_norm)
    return pl.pallas_call(
        kernel,
        grid=(batch // PAIR, n_tiles),
        in_specs=[
            pl.BlockSpec((PAIR, TILE, D_MODEL), lambda b, t: (b, t, 0)),
            pl.BlockSpec((PAIR, FFN_HALO, D_MODEL),
                         lambda b, t: (b, jnp.maximum(t * per_tile - 1, 0), 0)),
            pl.BlockSpec((PAIR, FFN_HALO, D_MODEL),
                         lambda b, t: (b, jnp.minimum((t + 1) * per_tile, halo_blocks - 1), 0)),
            pl.BlockSpec((PAIR, 6, D_MODEL), _mod_pair(n_latent_tiles, batch)),
            _resident((1, D_MODEL)),
            _resident((D_MODEL, FFN_HIDDEN)),
            _resident((D_MODEL, FFN_HIDDEN)),
            _resident((FFN_HIDDEN, D_MODEL)),
            _resident((FFN_KERNEL, FFN_HIDDEN)),
            _resident((1, FFN_HIDDEN)),
            _resident((1, D_MODEL)),
        ],
        out_specs=pl.BlockSpec((PAIR, TILE, D_MODEL), lambda b, t: (b, t, 0)),
        out_shape=jax.ShapeDtypeStruct((batch, tokens, D_MODEL), F32),
        scratch_shapes=[
            pltpu.VMEM((PAIR, TILE + 2 * FFN_HALO, D_MODEL), F32),
            pltpu.VMEM((PAIR, TILE + 2 * FFN_HALO, FFN_HIDDEN), F32),
            pltpu.VMEM((PAIR * TILE, FFN_HIDDEN), BF16),
        ],
        compiler_params=_params(("parallel", "parallel")),
        name="ffn",
    )(x1, x1, x1, mods, g, wg, wv, wd, dw, dwb, fg)


def _rope_tables(n_latent, n_ctx):
    pos = jnp.arange(n_latent, dtype=jnp.int32)
    rc = jnp.stack([pos // GRID_W, pos % GRID_W], axis=0).astype(F32)
    n_freq = HEAD_DIM // 4
    inv_freq = ROPE_THETA ** (-jnp.arange(n_freq, dtype=F32) / n_freq)
    ang = rc[:, None, :] * inv_freq[None, :, None]
    cos, sin = jnp.cos(ang), jnp.sin(ang)
    cos_t = jnp.concatenate([cos, cos], axis=1).reshape(HEAD_DIM, n_latent)
    sin_t = jnp.concatenate([-sin, sin], axis=1).reshape(HEAD_DIM, n_latent)
    cos_t = jnp.concatenate([cos_t, jnp.ones((HEAD_DIM, n_ctx), F32)], axis=1)
    sin_t = jnp.concatenate([sin_t, jnp.zeros((HEAD_DIM, n_ctx), F32)], axis=1)
    return cos_t, sin_t


def kernel(x, c, ctx, c_ctx, w_mod, b_mod, norm1_g, norm2_g, w_in, q_norm_g, k_norm_g,
           conv_dw, conv_dw_b, conv_ln_g, conv_ln_b, w_out, ffn_w_up, ffn_dw, ffn_dw_b,
           ffn_w_down, final_g):
    batch, seq, _ = x.shape
    n_ctx = ctx.shape[1]
    depth = w_mod.shape[0]
    assert seq % TILE == 0 and n_ctx == TILE and batch % PAIR == 0 and batch + PAIR <= MOD_ROWS
    n_latent_tiles = seq // TILE
    n_tiles = n_latent_tiles + 1

    cc = jnp.concatenate([c, jnp.broadcast_to(c_ctx[None, :], (PAIR, D_MODEL)),
                          jnp.zeros((MOD_ROWS - batch - PAIR, D_MODEL), F32)], axis=0)
    mods = _modulation(cc, w_mod, b_mod).reshape(depth, MOD_ROWS, 6, D_MODEL)
    cos_t, sin_t = _rope_tables(seq, n_ctx)
    x_lat, x_ctx = x, ctx

    for l in range(depth):
        last = l == depth - 1
        wqkvt = w_in[l, :, :QKV_WIDTH].T.astype(BF16)
        wconv = w_in[l, :, QKV_WIDTH:].astype(BF16)
        gq = jnp.broadcast_to(q_norm_g[l][:, None], (HEAD_DIM, PAIR * TILE))
        gk = jnp.broadcast_to(k_norm_g[l][:, None], (HEAD_DIM, PAIR * TILE))
        qt, k, vt, glu = _inproj(x_lat, x_ctx, mods[l], norm1_g[l][None, :], wqkvt, wconv, gq, gk,
                                 cos_t, sin_t, n_latent_tiles)
        n_out_tiles = n_latent_tiles if last else n_tiles
        attn = _attention(qt, k, vt, n_latent_tiles, n_out_tiles)
        dw = jnp.concatenate([conv_dw[l], jnp.zeros((1, CONV_WIDTH), F32)], axis=0)
        x1 = _outproj(glu, attn, x_lat, x_ctx, mods[l], dw, conv_dw_b[l][None, :], conv_ln_g[l][None, :],
                      conv_ln_b[l][None, :], w_out[l].astype(BF16), n_latent_tiles, n_out_tiles)
        x_lat = x_ctx = _ffn(x1, mods[l], norm2_g[l][None, :], ffn_w_up[l, :, :FFN_HIDDEN].astype(BF16),
                             ffn_w_up[l, :, FFN_HIDDEN:].astype(BF16), ffn_w_down[l].astype(BF16),
                             ffn_dw[l], ffn_dw_b[l][None, :], final_g[None, :], n_latent_tiles, last)
    return x_lat
```

```python
import functools

import jax
import jax.numpy as jnp
from jax import lax
from jax.experimental import pallas as pl
from jax.experimental.pallas import tpu as pltpu

F32 = jnp.float32
BF16 = jnp.bfloat16

D_MODEL = 1024
HEAD_DIM = 64
N_Q_HEADS = 8
N_KV_HEADS = 2
GROUP = N_Q_HEADS // N_KV_HEADS
ATTN_WIDTH = N_Q_HEADS * HEAD_DIM
KV_WIDTH = N_KV_HEADS * HEAD_DIM
QKV_WIDTH = ATTN_WIDTH + 2 * KV_WIDTH
CONV_WIDTH = D_MODEL - ATTN_WIDTH
CONV_KERNEL = 31
CONV_PAD = CONV_KERNEL // 2
FFN_HIDDEN = 2816
FFN_KERNEL = 3
GRID_W = 64
ROPE_THETA = 10000.0
EPS = 1e-6

LANES = 128
SUBLANES = 8
BF16_ROWS = 16
Q_SCALE = HEAD_DIM ** -0.5 * 1.4426950408889634
TILE = 256
KEY_BLOCK = 256
PAIR = 2
CONV_HALO = 16
FFN_HALO = 8
FFN_CHUNK = 256
MOD_ROWS = 24
MOD_TN = 1024
VMEM_LIMIT = 56 * 1024 * 1024

NT_DIMS = (((1,), (1,)), ((), ()))


def _params(sem, vmem=VMEM_LIMIT):
    return pltpu.CompilerParams(dimension_semantics=sem, vmem_limit_bytes=vmem)


def _mod_kernel(c_ref, w_ref, b_ref, o_ref):
    c = c_ref[...]
    s = c * jax.nn.sigmoid(c)
    o_ref[0] = jnp.dot(s, w_ref[0], preferred_element_type=F32,
                       precision=lax.Precision.HIGHEST) + b_ref[0]


def _modulation(cc, w_mod, b_mod):
    depth = w_mod.shape[0]
    n = w_mod.shape[2]
    return pl.pallas_call(
        _mod_kernel,
        grid=(depth, n // MOD_TN),
        in_specs=[
            pl.BlockSpec((MOD_ROWS, D_MODEL), lambda l, j: (0, 0)),
            pl.BlockSpec((1, D_MODEL, MOD_TN), lambda l, j: (l, 0, j)),
            pl.BlockSpec((1, 1, MOD_TN), lambda l, j: (l, 0, j)),
        ],
        out_specs=pl.BlockSpec((1, MOD_ROWS, MOD_TN), lambda l, j: (l, 0, j)),
        out_shape=jax.ShapeDtypeStruct((depth, MOD_ROWS, n), F32),
        compiler_params=_params(("parallel", "parallel")),
        name="modulation",
    )(cc, w_mod, b_mod.reshape(depth, 1, n))


def _rms_mod(x, g, scale, shift):
    ms = jnp.mean(x * x, axis=-1, keepdims=True)
    return (x * lax.rsqrt(ms + EPS)) * g * (1.0 + scale) + shift


def _stream_tile(lat_ref, ctx_ref, n_latent_tiles, e):
    return jnp.where(pl.program_id(1) < n_latent_tiles, lat_ref[e], ctx_ref[e])


def _stream_specs(lat, ctx, n_latent_tiles):
    ctx_block = 0 if ctx.shape[1] == TILE else n_latent_tiles
    return [
        pl.BlockSpec((PAIR, TILE, D_MODEL), lambda b, t: (b, jnp.minimum(t, n_latent_tiles - 1), 0)),
        pl.BlockSpec((PAIR, TILE, D_MODEL), lambda b, t: (b, ctx_block, 0)),
    ]


def _mod_pair(n_latent_tiles, batch):
    return lambda b, t: (jnp.where(t < n_latent_tiles, b, batch // PAIR), 0, 0)


def _resident(shape):
    return pl.BlockSpec(shape, lambda b, t: (0,) * len(shape), pipeline_mode=pl.Buffered(1))


def _layer_weight(shape, layer, col_block=0):
    return pl.BlockSpec((None,) + shape, lambda b, t: (layer, 0, col_block), pipeline_mode=pl.Buffered(1))


def _inproj_kernel(xl_ref, xc_ref, mod_ref, g_ref, wqkvt_ref, wconv_ref, gq_ref, gk_ref, cos_ref, sin_ref,
                   qt_ref, k_ref, vt_ref, glu_ref, *, n_latent_tiles):
    cos = cos_ref[...]
    sin = sin_ref[...]
    gq = gq_ref[...]
    gk = gk_ref[...]

    def project(e):
        h = _rms_mod(_stream_tile(xl_ref, xc_ref, n_latent_tiles, e), g_ref[...],
                     mod_ref[e, 1:2, :], mod_ref[e, 0:1, :]).astype(BF16)
        u = jnp.dot(h, wconv_ref[...], preferred_element_type=F32)
        qkvt = lax.dot_general(wqkvt_ref[...], h, NT_DIMS, preferred_element_type=F32)
        return u, qkvt

    def finish(e, u, qkvt):
        glu_ref[e] = u[:, :CONV_WIDTH] * jax.nn.sigmoid(u[:, CONV_WIDTH:])

        def head(row0, gain, scale):
            blk = qkvt[row0:row0 + HEAD_DIM, :]
            ms = jnp.mean(blk * blk, axis=0, keepdims=True)
            n = blk * lax.rsqrt(ms + EPS) * gain
            swapped = jnp.concatenate([n[16:32], n[0:16], n[48:64], n[32:48]], axis=0)
            return (n * cos + swapped * sin) * scale

        qt = jnp.concatenate([head(i * HEAD_DIM, gq, Q_SCALE) for i in range(N_Q_HEADS)], axis=0)
        kt = jnp.concatenate([head(ATTN_WIDTH + i * HEAD_DIM, gk, 1.0) for i in range(N_KV_HEADS)], axis=0)
        qt_ref[e] = qt.astype(BF16)
        k_ref[e] = kt.T.astype(BF16)
        vt_ref[e] = qkvt[ATTN_WIDTH + KV_WIDTH:, :].astype(BF16)

    projected = [project(e) for e in range(PAIR)]
    for e in range(PAIR):
        finish(e, *projected[e])


def _inproj(x_lat, x_ctx, mods, g, wqkvt, wconv, gq, gk, cos_t, sin_t, n_latent_tiles, layer):
    batch = x_lat.shape[0]
    n_tiles = n_latent_tiles + 1
    tokens = n_tiles * TILE
    return pl.pallas_call(
        functools.partial(_inproj_kernel, n_latent_tiles=n_latent_tiles),
        grid=(batch // PAIR, n_tiles),
        in_specs=_stream_specs(x_lat, x_ctx, n_latent_tiles) + [
            pl.BlockSpec((PAIR, 6, D_MODEL), _mod_pair(n_latent_tiles, batch)),
            _resident((1, D_MODEL)),
            _layer_weight((QKV_WIDTH, D_MODEL), layer),
            _layer_weight((D_MODEL, 2 * CONV_WIDTH), layer),
            _resident((HEAD_DIM, TILE)),
            _resident((HEAD_DIM, TILE)),
            pl.BlockSpec((HEAD_DIM, TILE), lambda b, t: (0, t)),
            pl.BlockSpec((HEAD_DIM, TILE), lambda b, t: (0, t)),
        ],
        out_specs=[
            pl.BlockSpec((PAIR, ATTN_WIDTH, TILE), lambda b, t: (b, 0, t)),
            pl.BlockSpec((PAIR, TILE, KV_WIDTH), lambda b, t: (b, t, 0)),
            pl.BlockSpec((PAIR, KV_WIDTH, TILE), lambda b, t: (b, 0, t)),
            pl.BlockSpec((PAIR, TILE, CONV_WIDTH), lambda b, t: (b, t, 0)),
        ],
        out_shape=[
            jax.ShapeDtypeStruct((batch, ATTN_WIDTH, tokens), BF16),
            jax.ShapeDtypeStruct((batch, tokens, KV_WIDTH), BF16),
            jax.ShapeDtypeStruct((batch, KV_WIDTH, tokens), BF16),
            jax.ShapeDtypeStruct((batch, tokens, CONV_WIDTH), F32),
        ],
        compiler_params=_params(("parallel", "parallel")),
        name="inproj",
    )(x_lat, x_ctx, mods, g, wqkvt, wconv, gq, gk, cos_t, sin_t)


def _key_blocks(key0, key1):
    return [slice(k, min(k + KEY_BLOCK, key1)) for k in range(key0, key1, KEY_BLOCK)]


def _attend(qt_ref, k_ref, vt_ref, o_ref, key0, key1):
    def query_operand(g):
        zero = jnp.zeros((HEAD_DIM, TILE), BF16)
        cols = []
        for a in range(GROUP):
            qa = qt_ref[0, (g * GROUP + a) * HEAD_DIM:(g * GROUP + a + 1) * HEAD_DIM, :]
            cols.append(jnp.concatenate([qa if j == g else zero for j in range(N_KV_HEADS)], axis=0))
        return jnp.concatenate(cols, axis=1)

    w = [query_operand(g) for g in range(N_KV_HEADS)]

    blocks = _key_blocks(key0, key1)
    n_blocks = len(blocks)

    def scores(g, i):
        return jnp.dot(k_ref[0, blocks[i], :], w[g], preferred_element_type=F32)

    def values(g, i, p):
        ones = jnp.ones((BF16_ROWS, p.shape[0]), BF16)
        v1 = jnp.concatenate([vt_ref[0, g * HEAD_DIM:(g + 1) * HEAD_DIM, blocks[i]], ones], axis=0)
        return jnp.dot(v1, p, preferred_element_type=F32)

    items = [(g, i) for g in range(N_KV_HEADS) for i in range(n_blocks)]
    m = [None] * N_KV_HEADS
    acc = [None] * N_KV_HEADS

    def fold(g, i, p, alpha):
        v = values(g, i, p)
        acc[g] = v if i == 0 else alpha * acc[g] + v

    s_next = scores(*items[0])
    pending = None
    for n, (g, i) in enumerate(items):
        s = s_next
        if n + 1 < len(items):
            s_next = scores(*items[n + 1])
        if pending is not None:
            fold(*pending)
        alpha = None
        if i == 0:
            m[g] = jnp.max(s, axis=0, keepdims=True)
        else:
            m_new = jnp.maximum(m[g], jnp.max(s, axis=0, keepdims=True))
            alpha = jnp.exp2(m[g] - m_new)
            m[g] = m_new
        pending = (g, i, jnp.exp2(s - m[g]).astype(BF16), alpha)
    fold(*pending)

    heads = []
    for g in range(N_KV_HEADS):
        ot = acc[g][:HEAD_DIM] / acc[g][HEAD_DIM:HEAD_DIM + 1]
        heads += [ot[:, a * TILE:(a + 1) * TILE] for a in range(GROUP)]
    o_ref[0] = jnp.concatenate(heads, axis=0).T.astype(BF16)


def _attn_kernel(qt_ref, k_ref, vt_ref, o_ref, *, n_latent_tiles, n_tokens):
    t = pl.program_id(1)
    n_latent = n_latent_tiles * TILE

    @pl.when(t < n_latent_tiles)
    def _():
        _attend(qt_ref, k_ref, vt_ref, o_ref, 0, n_tokens)

    @pl.when(t >= n_latent_tiles)
    def _():
        _attend(qt_ref, k_ref, vt_ref, o_ref, n_latent, n_tokens)


def _attention(qt, k, vt, n_latent_tiles, n_q_tiles):
    batch, tokens, _ = k.shape
    kernel = functools.partial(_attn_kernel, n_latent_tiles=n_latent_tiles, n_tokens=tokens)
    return pl.pallas_call(
        kernel,
        grid=(batch, n_q_tiles),
        in_specs=[
            pl.BlockSpec((1, ATTN_WIDTH, TILE), lambda b, t: (b, 0, t)),
            pl.BlockSpec((1, tokens, KV_WIDTH), lambda b, t: (b, 0, 0)),
            pl.BlockSpec((1, KV_WIDTH, tokens), lambda b, t: (b, 0, 0)),
        ],
        out_specs=pl.BlockSpec((1, TILE, ATTN_WIDTH), lambda b, t: (b, t, 0)),
        out_shape=jax.ShapeDtypeStruct((batch, n_q_tiles * TILE, ATTN_WIDTH), BF16),
        compiler_params=_params(("parallel", "parallel")),
        name="attention",
    )(qt, k, vt)


def _depthwise31(win_ref, dw_ref):
    half = TILE // 2
    span = half + (CONV_KERNEL // SUBLANES) * SUBLANES
    load = span + SUBLANES
    cols = []
    for c in range(CONV_WIDTH // LANES):
        lanes = slice(c * LANES, (c + 1) * LANES)
        rows = []
        for r0 in range(0, TILE, half):
            acc = jnp.zeros((half, LANES), F32)
            block = win_ref[r0:r0 + load, lanes]
            for r in range(SUBLANES):
                shifted = block if r == 0 else pltpu.roll(block, load - r, 0)
                for a in range(span // SUBLANES - half // SUBLANES + 1):
                    j = a * SUBLANES + r - (CONV_HALO - CONV_PAD)
                    if 0 <= j < CONV_KERNEL:
                        acc = acc + shifted[a * SUBLANES:a * SUBLANES + half] * dw_ref[j:j + 1, lanes]
            rows.append(acc)
        cols.append(jnp.concatenate(rows, axis=0))
    return jnp.concatenate(cols, axis=1)


def _outproj_kernel(glu_ref, glu_prev_ref, glu_next_ref, attn_ref, xl_ref, xc_ref, mod_ref, dw_ref, dwb_ref,
                    lng_ref, lnb_ref, wout_ref, o_ref, win_ref, *, n_latent_tiles):
    t = pl.program_id(1)
    has_prev = jnp.logical_and(t != 0, t != n_latent_tiles)
    has_next = jnp.logical_and(t != n_latent_tiles - 1, t != n_latent_tiles)
    attn_proj = [jnp.dot(attn_ref[e], wout_ref[:ATTN_WIDTH, :], preferred_element_type=F32)
                 for e in range(PAIR)]
    for e in range(PAIR):
        win = win_ref.at[e]
        win[0:CONV_HALO, :] = jnp.where(has_prev, glu_prev_ref[e], 0.0)
        win[CONV_HALO:CONV_HALO + TILE, :] = glu_ref[e]
        win[CONV_HALO + TILE:, :] = jnp.where(has_next, glu_next_ref[e], 0.0)
        y = _depthwise31(win, dw_ref) + dwb_ref[...]

        mu = jnp.mean(y, axis=-1, keepdims=True)
        d = y - mu
        var = jnp.mean(d * d, axis=-1, keepdims=True)
        z = d * lax.rsqrt(var + EPS) * lng_ref[...] + lnb_ref[...]
        conv = (z * jax.nn.sigmoid(z)).astype(BF16)

        proj = attn_proj[e] + jnp.dot(conv, wout_ref[ATTN_WIDTH:, :], preferred_element_type=F32)
        o_ref[e] = _stream_tile(xl_ref, xc_ref, n_latent_tiles, e) + mod_ref[e, 2:3, :] * proj


def _outproj(glu, attn, x_lat, x_ctx, mods, dw, dwb, lng, lnb, wout, n_latent_tiles, n_tiles, layer):
    batch = x_lat.shape[0]
    halo_blocks = glu.shape[1] // CONV_HALO
    per_tile = TILE // CONV_HALO
    kernel = functools.partial(_outproj_kernel, n_latent_tiles=n_latent_tiles)
    return pl.pallas_call(
        kernel,
        grid=(batch // PAIR, n_tiles),
        in_specs=[
            pl.BlockSpec((PAIR, TILE, CONV_WIDTH), lambda b, t: (b, t, 0)),
            pl.BlockSpec((PAIR, CONV_HALO, CONV_WIDTH),
                         lambda b, t: (b, jnp.maximum(t * per_tile - 1, 0), 0)),
            pl.BlockSpec((PAIR, CONV_HALO, CONV_WIDTH),
                         lambda b, t: (b, jnp.minimum((t + 1) * per_tile, halo_blocks - 1), 0)),
            pl.BlockSpec((PAIR, TILE, ATTN_WIDTH), lambda b, t: (b, t, 0)),
        ] + _stream_specs(x_lat, x_ctx, n_latent_tiles) + [
            pl.BlockSpec((PAIR, 6, D_MODEL), _mod_pair(n_latent_tiles, batch)),
            _resident((CONV_KERNEL + 1, CONV_WIDTH)),
            _resident((1, CONV_WIDTH)),
            _resident((1, CONV_WIDTH)),
            _resident((1, CONV_WIDTH)),
            _layer_weight((D_MODEL, D_MODEL), layer),
        ],
        out_specs=pl.BlockSpec((PAIR, TILE, D_MODEL), lambda b, t: (b, t, 0)),
        out_shape=jax.ShapeDtypeStruct((batch, n_tiles * TILE, D_MODEL), F32),
        scratch_shapes=[pltpu.VMEM((PAIR, TILE + 2 * CONV_HALO, CONV_WIDTH), F32)],
        compiler_params=_params(("parallel", "parallel")),
        name="outproj",
    )(glu, glu, glu, attn, x_lat, x_ctx, mods, dw, dwb, lng, lnb, wout)


def _ffn_kernel(x_ref, x_prev_ref, x_next_ref, mod_ref, g_ref, wg_ref, wv_ref, wd_ref, dw_ref, dwb_ref,
                fg_ref, o_ref, xw_ref, gate_ref, act_ref, *, n_latent_tiles, final_norm):
    t = pl.program_id(1)
    has_prev = jnp.logical_and(t != 0, t != n_latent_tiles)
    has_next = jnp.logical_and(t != n_latent_tiles - 1, t != n_latent_tiles)
    halo0 = PAIR * TILE

    tiles, halos = [], []
    for e in range(PAIR):
        xw_ref[e, 0:TILE, :] = x_ref[e]
        xw_ref[e, TILE:TILE + FFN_HALO, :] = x_next_ref[e]
        xw_ref[e, TILE + FFN_HALO:, :] = x_prev_ref[e]
        he = _rms_mod(xw_ref[e], g_ref[...], mod_ref[e, 4:5, :], mod_ref[e, 3:4, :]).astype(BF16)
        tiles.append(he[0:TILE])
        halos.append(he[TILE:])
    h = jnp.concatenate(tiles + halos, axis=0)
    h_tiles = h[0:halo0]

    for c in range(FFN_HIDDEN // FFN_CHUNK):
        cols = slice(c * FFN_CHUNK, (c + 1) * FFN_CHUNK)
        gate = jnp.dot(h, wg_ref[:, cols], preferred_element_type=F32)
        val = jnp.dot(h_tiles, wv_ref[:, cols], preferred_element_type=F32)
        for e in range(PAIR):
            nxt = halo0 + 2 * FFN_HALO * e
            gate_ref[e, 0:FFN_HALO, cols] = jnp.where(has_prev, gate[nxt + FFN_HALO:nxt + 2 * FFN_HALO], 0.0)
            gate_ref[e, FFN_HALO:FFN_HALO + TILE, cols] = gate[e * TILE:(e + 1) * TILE]
            gate_ref[e, FFN_HALO + TILE:, cols] = jnp.where(has_next, gate[nxt:nxt + FFN_HALO], 0.0)
            y = dwb_ref[:, cols]
            for j in range(FFN_KERNEL):
                start = FFN_HALO - FFN_KERNEL // 2 + j
                y = y + gate_ref[e, start:start + TILE, cols] * dw_ref[j:j + 1, cols]
            act_ref[e * TILE:(e + 1) * TILE, cols] = \
                (y * jax.nn.sigmoid(y) * val[e * TILE:(e + 1) * TILE]).astype(BF16)

    down = jnp.dot(act_ref[...], wd_ref[...], preferred_element_type=F32)
    for e in range(PAIR):
        out = x_ref[e] + mod_ref[e, 5:6, :] * down[e * TILE:(e + 1) * TILE]
        if final_norm:
            ms = jnp.mean(out * out, axis=-1, keepdims=True)
            out = out * lax.rsqrt(ms + EPS) * fg_ref[...]
        o_ref[e] = out


def _ffn(x1, mods, g, w_up, wd, dw, dwb, fg, n_latent_tiles, final_norm, layer):
    batch, tokens, _ = x1.shape
    n_tiles = tokens // TILE
    halo_blocks = tokens // FFN_HALO
    per_tile = TILE // FFN_HALO
    kernel = functools.partial(_ffn_kernel, n_latent_tiles=n_latent_tiles, final_norm=final_norm)
    return pl.pallas_call(
        kernel,
        grid=(batch // PAIR, n_tiles),
        in_specs=[
            pl.BlockSpec((PAIR, TILE, D_MODEL), lambda b, t: (b, t, 0)),
            pl.BlockSpec((PAIR, FFN_HALO, D_MODEL),
                         lambda b, t: (b, jnp.maximum(t * per_tile - 1, 0), 0)),
            pl.BlockSpec((PAIR, FFN_HALO, D_MODEL),
                         lambda b, t: (b, jnp.minimum((t + 1) * per_tile, halo_blocks - 1), 0)),
            pl.BlockSpec((PAIR, 6, D_MODEL), _mod_pair(n_latent_tiles, batch)),
            _resident((1, D_MODEL)),
            _layer_weight((D_MODEL, FFN_HIDDEN), layer, 0),
            _layer_weight((D_MODEL, FFN_HIDDEN), layer, 1),
            _layer_weight((FFN_HIDDEN, D_MODEL), layer),
            _resident((FFN_KERNEL, FFN_HIDDEN)),
            _resident((1, FFN_HIDDEN)),
            _resident((1, D_MODEL)),
        ],
        out_specs=pl.BlockSpec((PAIR, TILE, D_MODEL), lambda b, t: (b, t, 0)),
        out_shape=jax.ShapeDtypeStruct((batch, tokens, D_MODEL), F32),
        scratch_shapes=[
            pltpu.VMEM((PAIR, TILE + 2 * FFN_HALO, D_MODEL), F32),
            pltpu.VMEM((PAIR, TILE + 2 * FFN_HALO, FFN_HIDDEN), F32),
            pltpu.VMEM((PAIR * TILE, FFN_HIDDEN), BF16),
        ],
        compiler_params=_params(("parallel", "parallel")),
        name="ffn",
    )(x1, x1, x1, mods, g, w_up, w_up, wd, dw, dwb, fg)


def _rope_tables(n_latent, n_ctx):
    pos = jnp.arange(n_latent, dtype=jnp.int32)
    rc = jnp.stack([pos // GRID_W, pos % GRID_W], axis=0).astype(F32)
    n_freq = HEAD_DIM // 4
    inv_freq = ROPE_THETA ** (-jnp.arange(n_freq, dtype=F32) / n_freq)
    ang = rc[:, None, :] * inv_freq[None, :, None]
    cos, sin = jnp.cos(ang), jnp.sin(ang)
    cos_t = jnp.concatenate([cos, cos], axis=1).reshape(HEAD_DIM, n_latent)
    sin_t = jnp.concatenate([-sin, sin], axis=1).reshape(HEAD_DIM, n_latent)
    cos_t = jnp.concatenate([cos_t, jnp.ones((HEAD_DIM, n_ctx), F32)], axis=1)
    sin_t = jnp.concatenate([sin_t, jnp.zeros((HEAD_DIM, n_ctx), F32)], axis=1)
    return cos_t, sin_t


def kernel(x, c, ctx, c_ctx, w_mod, b_mod, norm1_g, norm2_g, w_in, q_norm_g, k_norm_g,
           conv_dw, conv_dw_b, conv_ln_g, conv_ln_b, w_out, ffn_w_up, ffn_dw, ffn_dw_b,
           ffn_w_down, final_g):
    batch, seq, _ = x.shape
    n_ctx = ctx.shape[1]
    depth = w_mod.shape[0]
    assert seq % TILE == 0 and n_ctx == TILE and batch % PAIR == 0 and batch + PAIR <= MOD_ROWS
    n_latent_tiles = seq // TILE
    n_tiles = n_latent_tiles + 1

    cc = jnp.concatenate([c, jnp.broadcast_to(c_ctx[None, :], (PAIR, D_MODEL)),
                          jnp.zeros((MOD_ROWS - batch - PAIR, D_MODEL), F32)], axis=0)
    mods = _modulation(cc, w_mod, b_mod).reshape(depth, MOD_ROWS, 6, D_MODEL)
    cos_t, sin_t = _rope_tables(seq, n_ctx)
    x_lat, x_ctx = x, ctx

    w_in_b = w_in.astype(BF16)
    wqkvt = jnp.swapaxes(w_in_b[:, :, :QKV_WIDTH], 1, 2)
    wconv = w_in_b[:, :, QKV_WIDTH:]
    w_out_b = w_out.astype(BF16)
    w_up_b = ffn_w_up.astype(BF16)
    w_down_b = ffn_w_down.astype(BF16)

    for l in range(depth):
        last = l == depth - 1
        gq = jnp.broadcast_to(q_norm_g[l][:, None], (HEAD_DIM, TILE))
        gk = jnp.broadcast_to(k_norm_g[l][:, None], (HEAD_DIM, TILE))
        qt, k, vt, glu = _inproj(x_lat, x_ctx, mods[l], norm1_g[l][None, :], wqkvt, wconv, gq, gk,
                                 cos_t, sin_t, n_latent_tiles, l)
        n_out_tiles = n_latent_tiles if last else n_tiles
        attn = _attention(qt, k, vt, n_latent_tiles, n_out_tiles)
        dw = jnp.concatenate([conv_dw[l], jnp.zeros((1, CONV_WIDTH), F32)], axis=0)
        x1 = _outproj(glu, attn, x_lat, x_ctx, mods[l], dw, conv_dw_b[l][None, :], conv_ln_g[l][None, :],
                      conv_ln_b[l][None, :], w_out_b, n_latent_tiles, n_out_tiles, l)
        x_lat = x_ctx = _ffn(x1, mods[l], norm2_g[l][None, :], w_up_b, w_down_b, ffn_dw[l],
                             ffn_dw_b[l][None, :], final_g[None, :], n_latent_tiles, last, l)
    return x_lat
```

```python
import functools

import jax
import jax.numpy as jnp
from jax import lax
from jax.experimental import pallas as pl
from jax.experimental.pallas import tpu as pltpu

F32 = jnp.float32
BF16 = jnp.bfloat16

D_MODEL = 1024
HEAD_DIM = 64
N_Q_HEADS = 8
N_KV_HEADS = 2
GROUP = N_Q_HEADS // N_KV_HEADS
ATTN_WIDTH = N_Q_HEADS * HEAD_DIM
KV_WIDTH = N_KV_HEADS * HEAD_DIM
QKV_WIDTH = ATTN_WIDTH + 2 * KV_WIDTH
CONV_WIDTH = D_MODEL - ATTN_WIDTH
CONV_KERNEL = 31
CONV_PAD = CONV_KERNEL // 2
FFN_HIDDEN = 2816
FFN_KERNEL = 3
GRID_W = 64
ROPE_THETA = 10000.0
EPS = 1e-6

LANES = 128
SUBLANES = 8
BF16_ROWS = 16
Q_SCALE = HEAD_DIM ** -0.5 * 1.4426950408889634
TILE = 256
KEY_BLOCK = 256
PAIR = 2
PROJ_ELEMS = 4
ATTN_ELEMS = 2
CONV_HALO = 16
FFN_HALO = 8
FFN_CHUNK = 256
MOD_ROWS = 24
MOD_TN = 1024
VMEM_LIMIT = 56 * 1024 * 1024

NT_DIMS = (((1,), (1,)), ((), ()))


def _params(sem, vmem=VMEM_LIMIT):
    return pltpu.CompilerParams(dimension_semantics=sem, vmem_limit_bytes=vmem)


def _mod_kernel(c_ref, w_ref, b_ref, o_ref):
    c = c_ref[...]
    s = c * jax.nn.sigmoid(c)
    o_ref[0] = jnp.dot(s, w_ref[0], preferred_element_type=F32,
                       precision=lax.Precision.HIGHEST) + b_ref[0]


def _modulation(cc, w_mod, b_mod):
    depth = w_mod.shape[0]
    n = w_mod.shape[2]
    return pl.pallas_call(
        _mod_kernel,
        grid=(depth, n // MOD_TN),
        in_specs=[
            pl.BlockSpec((MOD_ROWS, D_MODEL), lambda l, j: (0, 0)),
            pl.BlockSpec((1, D_MODEL, MOD_TN), lambda l, j: (l, 0, j)),
            pl.BlockSpec((1, 1, MOD_TN), lambda l, j: (l, 0, j)),
        ],
        out_specs=pl.BlockSpec((1, MOD_ROWS, MOD_TN), lambda l, j: (l, 0, j)),
        out_shape=jax.ShapeDtypeStruct((depth, MOD_ROWS, n), F32),
        compiler_params=_params(("parallel", "parallel")),
        name="modulation",
    )(cc, w_mod, b_mod.reshape(depth, 1, n))


def _rms_mod(x, g, scale, shift):
    ms = jnp.mean(x * x, axis=-1, keepdims=True)
    return (x * lax.rsqrt(ms + EPS)) * g * (1.0 + scale) + shift


def _stream_tile(lat_ref, ctx_ref, n_latent_tiles, e):
    return jnp.where(pl.program_id(1) < n_latent_tiles, lat_ref[e], ctx_ref[e])


def _stream_specs(lat, ctx, n_latent_tiles):
    ctx_block = 0 if ctx.shape[1] == TILE else n_latent_tiles
    return [
        pl.BlockSpec((PROJ_ELEMS, TILE, D_MODEL), lambda b, t: (b, jnp.minimum(t, n_latent_tiles - 1), 0)),
        pl.BlockSpec((PROJ_ELEMS, TILE, D_MODEL), lambda b, t: (b, ctx_block, 0)),
    ]


def _mod_spec(n_latent_tiles, batch, elems):
    return pl.BlockSpec((elems, 6, D_MODEL),
                        lambda b, t: (jnp.where(t < n_latent_tiles, b, batch // elems), 0, 0))


def _resident(shape):
    return pl.BlockSpec(shape, lambda b, t: (0,) * len(shape), pipeline_mode=pl.Buffered(1))


def _layer_weight(shape, layer, col_block=0):
    return pl.BlockSpec((None,) + shape, lambda b, t: (layer, 0, col_block), pipeline_mode=pl.Buffered(1))


def _inproj_kernel(xl_ref, xc_ref, mod_ref, g_ref, wqkvt_ref, wconv_ref, gq_ref, gk_ref, cos_ref, sin_ref,
                   qt_ref, k_ref, vt_ref, glu_ref, *, n_latent_tiles):
    cos = cos_ref[...]
    sin = sin_ref[...]
    gq = gq_ref[...]
    gk = gk_ref[...]

    def project(e):
        h = _rms_mod(_stream_tile(xl_ref, xc_ref, n_latent_tiles, e), g_ref[...],
                     mod_ref[e, 1:2, :], mod_ref[e, 0:1, :]).astype(BF16)
        u = jnp.dot(h, wconv_ref[...], preferred_element_type=F32)
        qkvt = lax.dot_general(wqkvt_ref[...], h, NT_DIMS, preferred_element_type=F32)
        return u, qkvt

    def finish(e, u, qkvt):
        glu_ref[e] = u[:, :CONV_WIDTH] * jax.nn.sigmoid(u[:, CONV_WIDTH:])

        def head(row0, gain, scale):
            blk = qkvt[row0:row0 + HEAD_DIM, :]
            ms = jnp.mean(blk * blk, axis=0, keepdims=True)
            n = blk * lax.rsqrt(ms + EPS) * gain
            swapped = jnp.concatenate([n[16:32], n[0:16], n[48:64], n[32:48]], axis=0)
            return (n * cos + swapped * sin) * scale

        qt = jnp.concatenate([head(i * HEAD_DIM, gq, Q_SCALE) for i in range(N_Q_HEADS)], axis=0)
        kt = jnp.concatenate([head(ATTN_WIDTH + i * HEAD_DIM, gk, 1.0) for i in range(N_KV_HEADS)], axis=0)
        qt_ref[e] = qt.astype(BF16)
        k_ref[e] = kt.T.astype(BF16)
        vt_ref[e] = qkvt[ATTN_WIDTH + KV_WIDTH:, :].astype(BF16)

    projected = project(0)
    for e in range(PROJ_ELEMS):
        current = projected
        if e + 1 < PROJ_ELEMS:
            projected = project(e + 1)
        finish(e, *current)


def _inproj(x_lat, x_ctx, mods, g, wqkvt, wconv, gq, gk, cos_t, sin_t, n_latent_tiles, layer):
    batch = x_lat.shape[0]
    n_tiles = n_latent_tiles + 1
    tokens = n_tiles * TILE
    return pl.pallas_call(
        functools.partial(_inproj_kernel, n_latent_tiles=n_latent_tiles),
        grid=(batch // PROJ_ELEMS, n_tiles),
        in_specs=_stream_specs(x_lat, x_ctx, n_latent_tiles) + [
            _mod_spec(n_latent_tiles, batch, PROJ_ELEMS),
            _resident((1, D_MODEL)),
            _layer_weight((QKV_WIDTH, D_MODEL), layer),
            _layer_weight((D_MODEL, 2 * CONV_WIDTH), layer),
            _resident((HEAD_DIM, TILE)),
            _resident((HEAD_DIM, TILE)),
            pl.BlockSpec((HEAD_DIM, TILE), lambda b, t: (0, t)),
            pl.BlockSpec((HEAD_DIM, TILE), lambda b, t: (0, t)),
        ],
        out_specs=[
            pl.BlockSpec((PROJ_ELEMS, ATTN_WIDTH, TILE), lambda b, t: (b, 0, t)),
            pl.BlockSpec((PROJ_ELEMS, TILE, KV_WIDTH), lambda b, t: (b, t, 0)),
            pl.BlockSpec((PROJ_ELEMS, KV_WIDTH, TILE), lambda b, t: (b, 0, t)),
            pl.BlockSpec((PROJ_ELEMS, TILE, CONV_WIDTH), lambda b, t: (b, t, 0)),
        ],
        out_shape=[
            jax.ShapeDtypeStruct((batch, ATTN_WIDTH, tokens), BF16),
            jax.ShapeDtypeStruct((batch, tokens, KV_WIDTH), BF16),
            jax.ShapeDtypeStruct((batch, KV_WIDTH, tokens), BF16),
            jax.ShapeDtypeStruct((batch, tokens, CONV_WIDTH), F32),
        ],
        compiler_params=_params(("parallel", "parallel")),
        name="inproj",
    )(x_lat, x_ctx, mods, g, wqkvt, wconv, gq, gk, cos_t, sin_t)


def _key_blocks(key0, key1):
    return [slice(k, min(k + KEY_BLOCK, key1)) for k in range(key0, key1, KEY_BLOCK)]


def _attend(qt_ref, k_ref, vt_ref, o_ref, key0, key1):
    def query_operand(e, g):
        zero = jnp.zeros((HEAD_DIM, TILE), BF16)
        cols = []
        for a in range(GROUP):
            qa = qt_ref[e, (g * GROUP + a) * HEAD_DIM:(g * GROUP + a + 1) * HEAD_DIM, :]
            cols.append(jnp.concatenate([qa if j == g else zero for j in range(N_KV_HEADS)], axis=0))
        return jnp.concatenate(cols, axis=1)

    streams = [(e, g) for e in range(ATTN_ELEMS) for g in range(N_KV_HEADS)]
    w = {st: query_operand(*st) for st in streams}

    blocks = _key_blocks(key0, key1)
    n_blocks = len(blocks)

    def scores(st, i):
        return jnp.dot(k_ref[st[0], blocks[i], :], w[st], preferred_element_type=F32)

    def values(st, i, p):
        e, g = st
        ones = jnp.ones((BF16_ROWS, p.shape[0]), BF16)
        v1 = jnp.concatenate([vt_ref[e, g * HEAD_DIM:(g + 1) * HEAD_DIM, blocks[i]], ones], axis=0)
        return jnp.dot(v1, p, preferred_element_type=F32)

    items = [(st, i) for st in streams for i in range(n_blocks)]
    m, acc, heads = {}, {}, {}

    def fold(st, i, p, alpha):
        v = values(st, i, p)
        acc[st] = v if i == 0 else alpha * acc[st] + v
        if i == n_blocks - 1:
            e, g = st
            ot = acc[st][:HEAD_DIM] / acc[st][HEAD_DIM:HEAD_DIM + 1]
            heads[st] = [ot[:, a * TILE:(a + 1) * TILE] for a in range(GROUP)]
            if g == N_KV_HEADS - 1:
                rows = sum((heads[(e, j)] for j in range(N_KV_HEADS)), [])
                o_ref[e] = jnp.concatenate(rows, axis=0).T.astype(BF16)

    s_next = scores(*items[0])
    pending = None
    for n, (st, i) in enumerate(items):
        s = s_next
        if n + 1 < len(items):
            s_next = scores(*items[n + 1])
        if pending is not None:
            fold(*pending)
        alpha = None
        if i == 0:
            m[st] = jnp.max(s, axis=0, keepdims=True)
        else:
            m_new = jnp.maximum(m[st], jnp.max(s, axis=0, keepdims=True))
            alpha = jnp.exp2(m[st] - m_new)
            m[st] = m_new
        pending = (st, i, jnp.exp2(s - m[st]).astype(BF16), alpha)
    fold(*pending)


def _attn_kernel(qt_ref, k_ref, vt_ref, o_ref, *, n_latent_tiles, n_tokens):
    t = pl.program_id(1)
    n_latent = n_latent_tiles * TILE

    @pl.when(t < n_latent_tiles)
    def _():
        _attend(qt_ref, k_ref, vt_ref, o_ref, 0, n_tokens)

    @pl.when(t >= n_latent_tiles)
    def _():
        _attend(qt_ref, k_ref, vt_ref, o_ref, n_latent, n_tokens)


def _attention(qt, k, vt, n_latent_tiles, n_q_tiles):
    batch, tokens, _ = k.shape
    kernel = functools.partial(_attn_kernel, n_latent_tiles=n_latent_tiles, n_tokens=tokens)
    return pl.pallas_call(
        kernel,
        grid=(batch // ATTN_ELEMS, n_q_tiles),
        in_specs=[
            pl.BlockSpec((ATTN_ELEMS, ATTN_WIDTH, TILE), lambda b, t: (b, 0, t)),
            pl.BlockSpec((ATTN_ELEMS, tokens, KV_WIDTH), lambda b, t: (b, 0, 0)),
            pl.BlockSpec((ATTN_ELEMS, KV_WIDTH, tokens), lambda b, t: (b, 0, 0)),
        ],
        out_specs=pl.BlockSpec((ATTN_ELEMS, TILE, ATTN_WIDTH), lambda b, t: (b, t, 0)),
        out_shape=jax.ShapeDtypeStruct((batch, n_q_tiles * TILE, ATTN_WIDTH), BF16),
        compiler_params=_params(("parallel", "parallel")),
        name="attention",
    )(qt, k, vt)


def _depthwise31(win_ref, dw_ref):
    half = TILE // 2
    span = half + (CONV_KERNEL // SUBLANES) * SUBLANES
    load = span + SUBLANES
    cols = []
    for c in range(CONV_WIDTH // LANES):
        lanes = slice(c * LANES, (c + 1) * LANES)
        rows = []
        for r0 in range(0, TILE, half):
            acc = jnp.zeros((half, LANES), F32)
            block = win_ref[r0:r0 + load, lanes]
            for r in range(SUBLANES):
                shifted = block if r == 0 else pltpu.roll(block, load - r, 0)
                for a in range(span // SUBLANES - half // SUBLANES + 1):
                    j = a * SUBLANES + r - (CONV_HALO - CONV_PAD)
                    if 0 <= j < CONV_KERNEL:
                        acc = acc + shifted[a * SUBLANES:a * SUBLANES + half] * dw_ref[j:j + 1, lanes]
            rows.append(acc)
        cols.append(jnp.concatenate(rows, axis=0))
    return jnp.concatenate(cols, axis=1)


def _outproj_kernel(glu_ref, glu_prev_ref, glu_next_ref, attn_ref, xl_ref, xc_ref, mod_ref, dw_ref, dwb_ref,
                    lng_ref, lnb_ref, wout_ref, o_ref, win_ref, *, n_latent_tiles):
    t = pl.program_id(1)
    has_prev = jnp.logical_and(t != 0, t != n_latent_tiles)
    has_next = jnp.logical_and(t != n_latent_tiles - 1, t != n_latent_tiles)
    for e in range(PROJ_ELEMS):
        win = win_ref.at[e]
        win[0:CONV_HALO, :] = jnp.where(has_prev, glu_prev_ref[e], 0.0)
        win[CONV_HALO:CONV_HALO + TILE, :] = glu_ref[e]
        win[CONV_HALO + TILE:, :] = jnp.where(has_next, glu_next_ref[e], 0.0)
        y = _depthwise31(win, dw_ref) + dwb_ref[...]

        mu = jnp.mean(y, axis=-1, keepdims=True)
        d = y - mu
        var = jnp.mean(d * d, axis=-1, keepdims=True)
        z = d * lax.rsqrt(var + EPS) * lng_ref[...] + lnb_ref[...]
        conv = (z * jax.nn.sigmoid(z)).astype(BF16)

        proj = jnp.dot(attn_ref[e], wout_ref[:ATTN_WIDTH, :], preferred_element_type=F32) \
            + jnp.dot(conv, wout_ref[ATTN_WIDTH:, :], preferred_element_type=F32)
        o_ref[e] = _stream_tile(xl_ref, xc_ref, n_latent_tiles, e) + mod_ref[e, 2:3, :] * proj


def _outproj(glu, attn, x_lat, x_ctx, mods, dw, dwb, lng, lnb, wout, n_latent_tiles, n_tiles, layer):
    batch = x_lat.shape[0]
    halo_blocks = glu.shape[1] // CONV_HALO
    per_tile = TILE // CONV_HALO
    kernel = functools.partial(_outproj_kernel, n_latent_tiles=n_latent_tiles)
    return pl.pallas_call(
        kernel,
        grid=(batch // PROJ_ELEMS, n_tiles),
        in_specs=[
            pl.BlockSpec((PROJ_ELEMS, TILE, CONV_WIDTH), lambda b, t: (b, t, 0)),
            pl.BlockSpec((PROJ_ELEMS, CONV_HALO, CONV_WIDTH),
                         lambda b, t: (b, jnp.maximum(t * per_tile - 1, 0), 0)),
            pl.BlockSpec((PROJ_ELEMS, CONV_HALO, CONV_WIDTH),
                         lambda b, t: (b, jnp.minimum((t + 1) * per_tile, halo_blocks - 1), 0)),
            pl.BlockSpec((PROJ_ELEMS, TILE, ATTN_WIDTH), lambda b, t: (b, t, 0)),
        ] + _stream_specs(x_lat, x_ctx, n_latent_tiles) + [
            _mod_spec(n_latent_tiles, batch, PROJ_ELEMS),
            _resident((CONV_KERNEL + 1, CONV_WIDTH)),
            _resident((1, CONV_WIDTH)),
            _resident((1, CONV_WIDTH)),
            _resident((1, CONV_WIDTH)),
            _layer_weight((D_MODEL, D_MODEL), layer),
        ],
        out_specs=pl.BlockSpec((PROJ_ELEMS, TILE, D_MODEL), lambda b, t: (b, t, 0)),
        out_shape=jax.ShapeDtypeStruct((batch, n_tiles * TILE, D_MODEL), F32),
        scratch_shapes=[pltpu.VMEM((PROJ_ELEMS, TILE + 2 * CONV_HALO, CONV_WIDTH), F32)],
        compiler_params=_params(("parallel", "parallel")),
        name="outproj",
    )(glu, glu, glu, attn, x_lat, x_ctx, mods, dw, dwb, lng, lnb, wout)


def _ffn_kernel(x_ref, x_prev_ref, x_next_ref, mod_ref, g_ref, wg_ref, wv_ref, wd_ref, dw_ref, dwb_ref,
                fg_ref, o_ref, xw_ref, gate_ref, act_ref, *, n_latent_tiles, final_norm):
    t = pl.program_id(1)
    has_prev = jnp.logical_and(t != 0, t != n_latent_tiles)
    has_next = jnp.logical_and(t != n_latent_tiles - 1, t != n_latent_tiles)
    halo0 = PAIR * TILE

    tiles, halos = [], []
    for e in range(PAIR):
        xw_ref[e, 0:TILE, :] = x_ref[e]
        xw_ref[e, TILE:TILE + FFN_HALO, :] = x_next_ref[e]
        xw_ref[e, TILE + FFN_HALO:, :] = x_prev_ref[e]
        he = _rms_mod(xw_ref[e], g_ref[...], mod_ref[e, 4:5, :], mod_ref[e, 3:4, :]).astype(BF16)
        tiles.append(he[0:TILE])
        halos.append(he[TILE:])
    h = jnp.concatenate(tiles + halos, axis=0)
    h_tiles = h[0:halo0]

    for c in range(FFN_HIDDEN // FFN_CHUNK):
        cols = slice(c * FFN_CHUNK, (c + 1) * FFN_CHUNK)
        gate = jnp.dot(h, wg_ref[:, cols], preferred_element_type=F32)
        val = jnp.dot(h_tiles, wv_ref[:, cols], preferred_element_type=F32)
        for e in range(PAIR):
            nxt = halo0 + 2 * FFN_HALO * e
            gate_ref[e, 0:FFN_HALO, cols] = jnp.where(has_prev, gate[nxt + FFN_HALO:nxt + 2 * FFN_HALO], 0.0)
            gate_ref[e, FFN_HALO:FFN_HALO + TILE, cols] = gate[e * TILE:(e + 1) * TILE]
            gate_ref[e, FFN_HALO + TILE:, cols] = jnp.where(has_next, gate[nxt:nxt + FFN_HALO], 0.0)
            y = dwb_ref[:, cols]
            for j in range(FFN_KERNEL):
                start = FFN_HALO - FFN_KERNEL // 2 + j
                y = y + gate_ref[e, start:start + TILE, cols] * dw_ref[j:j + 1, cols]
            act_ref[e * TILE:(e + 1) * TILE, cols] = \
                (y * jax.nn.sigmoid(y) * val[e * TILE:(e + 1) * TILE]).astype(BF16)

    down = jnp.dot(act_ref[...], wd_ref[...], preferred_element_type=F32)
    for e in range(PAIR):
        out = x_ref[e] + mod_ref[e, 5:6, :] * down[e * TILE:(e + 1) * TILE]
        if final_norm:
            ms = jnp.mean(out * out, axis=-1, keepdims=True)
            out = out * lax.rsqrt(ms + EPS) * fg_ref[...]
        o_ref[e] = out


def _ffn(x1, mods, g, w_up, wd, dw, dwb, fg, n_latent_tiles, final_norm, layer):
    batch, tokens, _ = x1.shape
    n_tiles = tokens // TILE
    halo_blocks = tokens // FFN_HALO
    per_tile = TILE // FFN_HALO
    kernel = functools.partial(_ffn_kernel, n_latent_tiles=n_latent_tiles, final_norm=final_norm)
    return pl.pallas_call(
        kernel,
        grid=(batch // PAIR, n_tiles),
        in_specs=[
            pl.BlockSpec((PAIR, TILE, D_MODEL), lambda b, t: (b, t, 0)),
            pl.BlockSpec((PAIR, FFN_HALO, D_MODEL),
                         lambda b, t: (b, jnp.maximum(t * per_tile - 1, 0), 0)),
            pl.BlockSpec((PAIR, FFN_HALO, D_MODEL),
                         lambda b, t: (b, jnp.minimum((t + 1) * per_tile, halo_blocks - 1), 0)),
            _mod_spec(n_latent_tiles, batch, PAIR),
            _resident((1, D_MODEL)),
            _layer_weight((D_MODEL, FFN_HIDDEN), layer, 0),
            _layer_weight((D_MODEL, FFN_HIDDEN), layer, 1),
            _layer_weight((FFN_HIDDEN, D_MODEL), layer),
            _resident((FFN_KERNEL, FFN_HIDDEN)),
            _resident((1, FFN_HIDDEN)),
            _resident((1, D_MODEL)),
        ],
        out_specs=pl.BlockSpec((PAIR, TILE, D_MODEL), lambda b, t: (b, t, 0)),
        out_shape=jax.ShapeDtypeStruct((batch, tokens, D_MODEL), F32),
        scratch_shapes=[
            pltpu.VMEM((PAIR, TILE + 2 * FFN_HALO, D_MODEL), F32),
            pltpu.VMEM((PAIR, TILE + 2 * FFN_HALO, FFN_HIDDEN), F32),
            pltpu.VMEM((PAIR * TILE, FFN_HIDDEN), BF16),
        ],
        compiler_params=_params(("parallel", "parallel")),
        name="ffn",
    )(x1, x1, x1, mods, g, w_up, w_up, wd, dw, dwb, fg)


def _rope_tables(n_latent, n_ctx):
    pos = jnp.arange(n_latent, dtype=jnp.int32)
    rc = jnp.stack([pos // GRID_W, pos % GRID_W], axis=0).astype(F32)
    n_freq = HEAD_DIM // 4
    inv_freq = ROPE_THETA ** (-jnp.arange(n_freq, dtype=F32) / n_freq)
    ang = rc[:, None, :] * inv_freq[None, :, None]
    cos, sin = jnp.cos(ang), jnp.sin(ang)
    cos_t = jnp.concatenate([cos, cos], axis=1).reshape(HEAD_DIM, n_latent)
    sin_t = jnp.concatenate([-sin, sin], axis=1).reshape(HEAD_DIM, n_latent)
    cos_t = jnp.concatenate([cos_t, jnp.ones((HEAD_DIM, n_ctx), F32)], axis=1)
    sin_t = jnp.concatenate([sin_t, jnp.zeros((HEAD_DIM, n_ctx), F32)], axis=1)
    return cos_t, sin_t


def kernel(x, c, ctx, c_ctx, w_mod, b_mod, norm1_g, norm2_g, w_in, q_norm_g, k_norm_g,
           conv_dw, conv_dw_b, conv_ln_g, conv_ln_b, w_out, ffn_w_up, ffn_dw, ffn_dw_b,
           ffn_w_down, final_g):
    batch, seq, _ = x.shape
    n_ctx = ctx.shape[1]
    depth = w_mod.shape[0]
    ctx_rows = max(PAIR, PROJ_ELEMS)
    assert seq % TILE == 0 and n_ctx == TILE and batch % ctx_rows == 0 and ctx_rows % PAIR == 0
    assert batch + ctx_rows <= MOD_ROWS and batch % ATTN_ELEMS == 0
    n_latent_tiles = seq // TILE
    n_tiles = n_latent_tiles + 1

    cc = jnp.concatenate([c, jnp.broadcast_to(c_ctx[None, :], (ctx_rows, D_MODEL)),
                          jnp.zeros((MOD_ROWS - batch - ctx_rows, D_MODEL), F32)], axis=0)
    mods = _modulation(cc, w_mod, b_mod).reshape(depth, MOD_ROWS, 6, D_MODEL)
    cos_t, sin_t = _rope_tables(seq, n_ctx)
    x_lat, x_ctx = x, ctx

    w_in_b = w_in.astype(BF16)
    wqkvt = jnp.swapaxes(w_in_b[:, :, :QKV_WIDTH], 1, 2)
    wconv = w_in_b[:, :, QKV_WIDTH:]
    w_out_b = w_out.astype(BF16)
    w_up_b = ffn_w_up.astype(BF16)
    w_down_b = ffn_w_down.astype(BF16)

    for l in range(depth):
        last = l == depth - 1
        gq = jnp.broadcast_to(q_norm_g[l][:, None], (HEAD_DIM, TILE))
        gk = jnp.broadcast_to(k_norm_g[l][:, None], (HEAD_DIM, TILE))
        qt, k, vt, glu = _inproj(x_lat, x_ctx, mods[l], norm1_g[l][None, :], wqkvt, wconv, gq, gk,
                                 cos_t, sin_t, n_latent_tiles, l)
        n_out_tiles = n_latent_tiles if last else n_tiles
        attn = _attention(qt, k, vt, n_latent_tiles, n_out_tiles)
        dw = jnp.concatenate([conv_dw[l], jnp.zeros((1, CONV_WIDTH), F32)], axis=0)
        x1 = _outproj(glu, attn, x_lat, x_ctx, mods[l], dw, conv_dw_b[l][None, :], conv_ln_g[l][None, :],
                      conv_ln_b[l][None, :], w_out_b, n_latent_tiles, n_out_tiles, l)
        x_lat = x_ctx = _ffn(x1, mods[l], norm2_g[l][None, :], w_up_b, w_down_b, ffn_dw[l],
                             ffn_dw_b[l][None, :], final_g[None, :], n_latent_tiles, last, l)
    return x_lat
```

```python
import functools

import jax
import jax.numpy as jnp
from jax import lax
from jax.experimental import pallas as pl
from jax.experimental.pallas import tpu as pltpu

F32 = jnp.float32
BF16 = jnp.bfloat16

D_MODEL = 1024
HEAD_DIM = 64
N_Q_HEADS = 8
N_KV_HEADS = 2
GROUP = N_Q_HEADS // N_KV_HEADS
ATTN_WIDTH = N_Q_HEADS * HEAD_DIM
KV_WIDTH = N_KV_HEADS * HEAD_DIM
QKV_WIDTH = ATTN_WIDTH + 2 * KV_WIDTH
CONV_WIDTH = D_MODEL - ATTN_WIDTH
CONV_KERNEL = 31
CONV_PAD = CONV_KERNEL // 2
FFN_HIDDEN = 2816
FFN_KERNEL = 3
GRID_W = 64
ROPE_THETA = 10000.0
EPS = 1e-6

LANES = 128
SUBLANES = 8
BF16_ROWS = 16
Q_SCALE = HEAD_DIM ** -0.5 * 1.4426950408889634
TILE = 256
KEY_BLOCK = 256
PAIR = 2
PROJ_ELEMS = 4
ATTN_ELEMS = 2
CONV_HALO = 16
WIN_STRIDE = 5
FFN_HALO = 8
FFN_CHUNK = 256
MOD_ROWS = 24
MOD_TN = 1024
VMEM_LIMIT = 56 * 1024 * 1024

NT_DIMS = (((1,), (1,)), ((), ()))


def _params(sem, vmem=VMEM_LIMIT):
    return pltpu.CompilerParams(dimension_semantics=sem, vmem_limit_bytes=vmem)


def _mod_kernel(c_ref, w_ref, b_ref, o_ref):
    c = c_ref[...]
    s = c * jax.nn.sigmoid(c)
    o_ref[0] = jnp.dot(s, w_ref[0], preferred_element_type=F32,
                       precision=lax.Precision.HIGHEST) + b_ref[0]


def _modulation(cc, w_mod, b_mod):
    depth = w_mod.shape[0]
    n = w_mod.shape[2]
    return pl.pallas_call(
        _mod_kernel,
        grid=(depth, n // MOD_TN),
        in_specs=[
            pl.BlockSpec((MOD_ROWS, D_MODEL), lambda l, j: (0, 0)),
            pl.BlockSpec((1, D_MODEL, MOD_TN), lambda l, j: (l, 0, j)),
            pl.BlockSpec((1, 1, MOD_TN), lambda l, j: (l, 0, j)),
        ],
        out_specs=pl.BlockSpec((1, MOD_ROWS, MOD_TN), lambda l, j: (l, 0, j)),
        out_shape=jax.ShapeDtypeStruct((depth, MOD_ROWS, n), F32),
        compiler_params=_params(("parallel", "parallel")),
        name="modulation",
    )(cc, w_mod, b_mod.reshape(depth, 1, n))


def _rms_mod(x, g, scale, shift):
    ms = jnp.mean(x * x, axis=-1, keepdims=True)
    return (x * lax.rsqrt(ms + EPS)) * g * (1.0 + scale) + shift


def _stream_tile(lat_ref, ctx_ref, n_latent_tiles, e):
    return jnp.where(pl.program_id(1) < n_latent_tiles, lat_ref[e], ctx_ref[e])


def _stream_specs(lat, ctx, n_latent_tiles):
    ctx_block = 0 if ctx.shape[1] == TILE else n_latent_tiles
    return [
        pl.BlockSpec((PROJ_ELEMS, TILE, D_MODEL), lambda b, t: (b, jnp.minimum(t, n_latent_tiles - 1), 0)),
        pl.BlockSpec((PROJ_ELEMS, TILE, D_MODEL), lambda b, t: (b, ctx_block, 0)),
    ]


def _mod_spec(n_latent_tiles, batch, elems):
    return pl.BlockSpec((elems, 6, D_MODEL),
                        lambda b, t: (jnp.where(t < n_latent_tiles, b, batch // elems), 0, 0))


def _resident(shape):
    return pl.BlockSpec(shape, lambda b, t: (0,) * len(shape), pipeline_mode=pl.Buffered(1))


def _layer_weight(shape, layer, col_block=0):
    return pl.BlockSpec((None,) + shape, lambda b, t: (layer, 0, col_block), pipeline_mode=pl.Buffered(1))


def _inproj_kernel(xl_ref, xc_ref, mod_ref, g_ref, wqkvt_ref, wconv_ref, gq_ref, gk_ref, cos_ref, sin_ref,
                   qt_ref, k_ref, vt_ref, glu_ref, *, n_latent_tiles):
    cos = cos_ref[...]
    sin = sin_ref[...]
    gq = gq_ref[...]
    gk = gk_ref[...]

    def project(e):
        h = _rms_mod(_stream_tile(xl_ref, xc_ref, n_latent_tiles, e), g_ref[...],
                     mod_ref[e, 1:2, :], mod_ref[e, 0:1, :]).astype(BF16)
        u = jnp.dot(h, wconv_ref[...], preferred_element_type=F32)
        qkvt = lax.dot_general(wqkvt_ref[...], h, NT_DIMS, preferred_element_type=F32)
        return u, qkvt

    def finish(e, u, qkvt):
        glu_ref[e] = u[:, :CONV_WIDTH] * jax.nn.sigmoid(u[:, CONV_WIDTH:])

        def head(row0, gain, scale):
            blk = qkvt[row0:row0 + HEAD_DIM, :]
            ms = jnp.mean(blk * blk, axis=0, keepdims=True)
            n = blk * lax.rsqrt(ms + EPS) * gain
            swapped = jnp.concatenate([n[16:32], n[0:16], n[48:64], n[32:48]], axis=0)
            return (n * cos + swapped * sin) * scale

        qt = jnp.concatenate([head(i * HEAD_DIM, gq, Q_SCALE) for i in range(N_Q_HEADS)], axis=0)
        kt = jnp.concatenate([head(ATTN_WIDTH + i * HEAD_DIM, gk, 1.0) for i in range(N_KV_HEADS)], axis=0)
        qt_ref[e] = qt.astype(BF16)
        k_ref[e] = kt.T.astype(BF16)
        vt_ref[e] = qkvt[ATTN_WIDTH + KV_WIDTH:, :].astype(BF16)

    projected = project(0)
    for e in range(PROJ_ELEMS):
        current = projected
        if e + 1 < PROJ_ELEMS:
            projected = project(e + 1)
        finish(e, *current)


def _inproj(x_lat, x_ctx, mods, g, wqkvt, wconv, gq, gk, cos_t, sin_t, n_latent_tiles, layer):
    batch = x_lat.shape[0]
    n_tiles = n_latent_tiles + 1
    tokens = n_tiles * TILE
    return pl.pallas_call(
        functools.partial(_inproj_kernel, n_latent_tiles=n_latent_tiles),
        grid=(batch // PROJ_ELEMS, n_tiles),
        in_specs=_stream_specs(x_lat, x_ctx, n_latent_tiles) + [
            _mod_spec(n_latent_tiles, batch, PROJ_ELEMS),
            _resident((1, D_MODEL)),
            _layer_weight((QKV_WIDTH, D_MODEL), layer),
            _layer_weight((D_MODEL, 2 * CONV_WIDTH), layer),
            _resident((HEAD_DIM, TILE)),
            _resident((HEAD_DIM, TILE)),
            pl.BlockSpec((HEAD_DIM, TILE), lambda b, t: (0, t)),
            pl.BlockSpec((HEAD_DIM, TILE), lambda b, t: (0, t)),
        ],
        out_specs=[
            pl.BlockSpec((PROJ_ELEMS, ATTN_WIDTH, TILE), lambda b, t: (b, 0, t)),
            pl.BlockSpec((PROJ_ELEMS, TILE, KV_WIDTH), lambda b, t: (b, t, 0)),
            pl.BlockSpec((PROJ_ELEMS, KV_WIDTH, TILE), lambda b, t: (b, 0, t)),
            pl.BlockSpec((PROJ_ELEMS, TILE, CONV_WIDTH), lambda b, t: (b, t, 0)),
        ],
        out_shape=[
            jax.ShapeDtypeStruct((batch, ATTN_WIDTH, tokens), BF16),
            jax.ShapeDtypeStruct((batch, tokens, KV_WIDTH), BF16),
            jax.ShapeDtypeStruct((batch, KV_WIDTH, tokens), BF16),
            jax.ShapeDtypeStruct((batch, tokens, CONV_WIDTH), F32),
        ],
        compiler_params=_params(("parallel", "parallel")),
        name="inproj",
    )(x_lat, x_ctx, mods, g, wqkvt, wconv, gq, gk, cos_t, sin_t)


def _key_blocks(key0, key1):
    return [slice(k, min(k + KEY_BLOCK, key1)) for k in range(key0, key1, KEY_BLOCK)]


def _attend(qt_ref, k_ref, vt_ref, o_ref, key0, key1):
    def query_operand(e, g):
        zero = jnp.zeros((HEAD_DIM, TILE), BF16)
        cols = []
        for a in range(GROUP):
            qa = qt_ref[e, (g * GROUP + a) * HEAD_DIM:(g * GROUP + a + 1) * HEAD_DIM, :]
            cols.append(jnp.concatenate([qa if j == g else zero for j in range(N_KV_HEADS)], axis=0))
        return jnp.concatenate(cols, axis=1)

    streams = [(e, g) for e in range(ATTN_ELEMS) for g in range(N_KV_HEADS)]
    w = {st: query_operand(*st) for st in streams}

    blocks = _key_blocks(key0, key1)
    n_blocks = len(blocks)

    def scores(st, i):
        return jnp.dot(k_ref[st[0], blocks[i], :], w[st], preferred_element_type=F32)

    def values(st, i, p):
        e, g = st
        ones = jnp.ones((BF16_ROWS, p.shape[0]), BF16)
        v1 = jnp.concatenate([vt_ref[e, g * HEAD_DIM:(g + 1) * HEAD_DIM, blocks[i]], ones], axis=0)
        return jnp.dot(v1, p, preferred_element_type=F32)

    items = [(st, i) for st in streams for i in range(n_blocks)]
    m, acc, heads = {}, {}, {}

    def fold(st, i, p, alpha):
        v = values(st, i, p)
        acc[st] = v if i == 0 else alpha * acc[st] + v
        if i == n_blocks - 1:
            e, g = st
            ot = acc[st][:HEAD_DIM] / acc[st][HEAD_DIM:HEAD_DIM + 1]
            heads[st] = [ot[:, a * TILE:(a + 1) * TILE] for a in range(GROUP)]
            if g == N_KV_HEADS - 1:
                rows = sum((heads[(e, j)] for j in range(N_KV_HEADS)), [])
                o_ref[e] = jnp.concatenate(rows, axis=0).T.astype(BF16)

    s_next = scores(*items[0])
    pending = None
    for n, (st, i) in enumerate(items):
        s = s_next
        if n + 1 < len(items):
            s_next = scores(*items[n + 1])
        if pending is not None:
            fold(*pending)
        alpha = None
        if i == 0:
            m[st] = jnp.max(s, axis=0, keepdims=True)
        else:
            m_new = jnp.maximum(m[st], jnp.max(s, axis=0, keepdims=True))
            alpha = jnp.exp2(m[st] - m_new)
            m[st] = m_new
        pending = (st, i, jnp.exp2(s - m[st]).astype(BF16), alpha)
    fold(*pending)


def _attn_kernel(qt_ref, k_ref, vt_ref, o_ref, *, n_latent_tiles, n_tokens):
    t = pl.program_id(1)
    n_latent = n_latent_tiles * TILE

    @pl.when(t < n_latent_tiles)
    def _():
        _attend(qt_ref, k_ref, vt_ref, o_ref, 0, n_tokens)

    @pl.when(t >= n_latent_tiles)
    def _():
        _attend(qt_ref, k_ref, vt_ref, o_ref, n_latent, n_tokens)


def _attention(qt, k, vt, n_latent_tiles, n_q_tiles):
    batch, tokens, _ = k.shape
    kernel = functools.partial(_attn_kernel, n_latent_tiles=n_latent_tiles, n_tokens=tokens)
    return pl.pallas_call(
        kernel,
        grid=(batch // ATTN_ELEMS, n_q_tiles),
        in_specs=[
            pl.BlockSpec((ATTN_ELEMS, ATTN_WIDTH, TILE), lambda b, t: (b, 0, t)),
            pl.BlockSpec((ATTN_ELEMS, tokens, KV_WIDTH), lambda b, t: (b, 0, 0)),
            pl.BlockSpec((ATTN_ELEMS, KV_WIDTH, tokens), lambda b, t: (b, 0, 0)),
        ],
        out_specs=pl.BlockSpec((ATTN_ELEMS, TILE, ATTN_WIDTH), lambda b, t: (b, t, 0)),
        out_shape=jax.ShapeDtypeStruct((batch, n_q_tiles * TILE, ATTN_WIDTH), BF16),
        compiler_params=_params(("parallel", "parallel")),
        name="attention",
    )(qt, k, vt)


def _depthwise31(win_ref, dw_ref):
    n_out = TILE // SUBLANES
    n_in = (TILE + 2 * CONV_HALO) // SUBLANES
    taps_a = -(-(CONV_KERNEL + CONV_HALO - CONV_PAD) // SUBLANES)
    cols = []
    for c in range(CONV_WIDTH // LANES):
        lanes = slice(c * LANES, (c + 1) * LANES)
        acc = [None] * n_out
        for i in range(n_in - 1):
            for r in range(SUBLANES):
                piece = win_ref[c, pl.ds(WIN_STRIDE * (i * SUBLANES + r), SUBLANES, stride=WIN_STRIDE), :]
                for a in range(taps_a):
                    o = i - a
                    j = a * SUBLANES + r - (CONV_HALO - CONV_PAD)
                    if 0 <= o < n_out and 0 <= j < CONV_KERNEL:
                        term = piece * dw_ref[j:j + 1, lanes]
                        acc[o] = term if acc[o] is None else acc[o] + term
        cols.append(jnp.concatenate(acc, axis=0))
    return jnp.concatenate(cols, axis=1)


def _outproj_kernel(glu_ref, glu_prev_ref, glu_next_ref, attn_ref, xl_ref, xc_ref, mod_ref, dw_ref, dwb_ref,
                    lng_ref, lnb_ref, wout_ref, o_ref, win_ref, *, n_latent_tiles):
    t = pl.program_id(1)
    has_prev = jnp.logical_and(t != 0, t != n_latent_tiles)
    has_next = jnp.logical_and(t != n_latent_tiles - 1, t != n_latent_tiles)
    for e in range(PROJ_ELEMS):
        win = win_ref.at[e]
        for c in range(CONV_WIDTH // LANES):
            lanes = slice(c * LANES, (c + 1) * LANES)

            def rows(first, n):
                return pl.ds(WIN_STRIDE * first, n, stride=WIN_STRIDE)

            win[c, rows(0, CONV_HALO), :] = jnp.where(has_prev, glu_prev_ref[e, :, lanes], 0.0)
            win[c, rows(CONV_HALO, TILE), :] = glu_ref[e, :, lanes]
            win[c, rows(CONV_HALO + TILE, CONV_HALO), :] = jnp.where(has_next, glu_next_ref[e, :, lanes], 0.0)
        y = _depthwise31(win, dw_ref) + dwb_ref[...]

        mu = jnp.mean(y, axis=-1, keepdims=True)
        d = y - mu
        var = jnp.mean(d * d, axis=-1, keepdims=True)
        z = d * lax.rsqrt(var + EPS) * lng_ref[...] + lnb_ref[...]
        conv = (z * jax.nn.sigmoid(z)).astype(BF16)

        proj = jnp.dot(attn_ref[e], wout_ref[:ATTN_WIDTH, :], preferred_element_type=F32) \
            + jnp.dot(conv, wout_ref[ATTN_WIDTH:, :], preferred_element_type=F32)
        o_ref[e] = _stream_tile(xl_ref, xc_ref, n_latent_tiles, e) + mod_ref[e, 2:3, :] * proj


def _outproj(glu, attn, x_lat, x_ctx, mods, dw, dwb, lng, lnb, wout, n_latent_tiles, n_tiles, layer):
    batch = x_lat.shape[0]
    halo_blocks = glu.shape[1] // CONV_HALO
    per_tile = TILE // CONV_HALO
    kernel = functools.partial(_outproj_kernel, n_latent_tiles=n_latent_tiles)
    return pl.pallas_call(
        kernel,
        grid=(batch // PROJ_ELEMS, n_tiles),
        in_specs=[
            pl.BlockSpec((PROJ_ELEMS, TILE, CONV_WIDTH), lambda b, t: (b, t, 0)),
            pl.BlockSpec((PROJ_ELEMS, CONV_HALO, CONV_WIDTH),
                         lambda b, t: (b, jnp.maximum(t * per_tile - 1, 0), 0)),
            pl.BlockSpec((PROJ_ELEMS, CONV_HALO, CONV_WIDTH),
                         lambda b, t: (b, jnp.minimum((t + 1) * per_tile, halo_blocks - 1), 0)),
            pl.BlockSpec((PROJ_ELEMS, TILE, ATTN_WIDTH), lambda b, t: (b, t, 0)),
        ] + _stream_specs(x_lat, x_ctx, n_latent_tiles) + [
            _mod_spec(n_latent_tiles, batch, PROJ_ELEMS),
            _resident((CONV_KERNEL + 1, CONV_WIDTH)),
            _resident((1, CONV_WIDTH)),
            _resident((1, CONV_WIDTH)),
            _resident((1, CONV_WIDTH)),
            _layer_weight((D_MODEL, D_MODEL), layer),
        ],
        out_specs=pl.BlockSpec((PROJ_ELEMS, TILE, D_MODEL), lambda b, t: (b, t, 0)),
        out_shape=jax.ShapeDtypeStruct((batch, n_tiles * TILE, D_MODEL), F32),
        scratch_shapes=[pltpu.VMEM((PROJ_ELEMS, CONV_WIDTH // LANES, WIN_STRIDE * (TILE + 2 * CONV_HALO), LANES),
                                   F32)],
        compiler_params=_params(("parallel", "parallel")),
        name="outproj",
    )(glu, glu, glu, attn, x_lat, x_ctx, mods, dw, dwb, lng, lnb, wout)


def _ffn_kernel(x_ref, x_prev_ref, x_next_ref, mod_ref, g_ref, wg_ref, wv_ref, wd_ref, dw_ref, dwb_ref,
                fg_ref, o_ref, xw_ref, gate_ref, act_ref, *, n_latent_tiles, final_norm):
    t = pl.program_id(1)
    has_prev = jnp.logical_and(t != 0, t != n_latent_tiles)
    has_next = jnp.logical_and(t != n_latent_tiles - 1, t != n_latent_tiles)
    halo0 = PAIR * TILE

    tiles, halos = [], []
    for e in range(PAIR):
        xw_ref[e, 0:TILE, :] = x_ref[e]
        xw_ref[e, TILE:TILE + FFN_HALO, :] = x_next_ref[e]
        xw_ref[e, TILE + FFN_HALO:, :] = x_prev_ref[e]
        he = _rms_mod(xw_ref[e], g_ref[...], mod_ref[e, 4:5, :], mod_ref[e, 3:4, :]).astype(BF16)
        tiles.append(he[0:TILE])
        halos.append(he[TILE:])
    h = jnp.concatenate(tiles + halos, axis=0)
    h_tiles = h[0:halo0]

    for c in range(FFN_HIDDEN // FFN_CHUNK):
        cols = slice(c * FFN_CHUNK, (c + 1) * FFN_CHUNK)
        gate = jnp.dot(h, wg_ref[:, cols], preferred_element_type=F32)
        val = jnp.dot(h_tiles, wv_ref[:, cols], preferred_element_type=F32)
        for e in range(PAIR):
            nxt = halo0 + 2 * FFN_HALO * e
            gate_ref[e, 0:FFN_HALO, cols] = jnp.where(has_prev, gate[nxt + FFN_HALO:nxt + 2 * FFN_HALO], 0.0)
            gate_ref[e, FFN_HALO:FFN_HALO + TILE, cols] = gate[e * TILE:(e + 1) * TILE]
            gate_ref[e, FFN_HALO + TILE:, cols] = jnp.where(has_next, gate[nxt:nxt + FFN_HALO], 0.0)
            y = dwb_ref[:, cols]
            for j in range(FFN_KERNEL):
                start = FFN_HALO - FFN_KERNEL // 2 + j
                y = y + gate_ref[e, start:start + TILE, cols] * dw_ref[j:j + 1, cols]
            act_ref[e * TILE:(e + 1) * TILE, cols] = \
                (y * jax.nn.sigmoid(y) * val[e * TILE:(e + 1) * TILE]).astype(BF16)

    down = jnp.dot(act_ref[...], wd_ref[...], preferred_element_type=F32)
    for e in range(PAIR):
        out = x_ref[e] + mod_ref[e, 5:6, :] * down[e * TILE:(e + 1) * TILE]
        if final_norm:
            ms = jnp.mean(out * out, axis=-1, keepdims=True)
            out = out * lax.rsqrt(ms + EPS) * fg_ref[...]
        o_ref[e] = out


def _ffn(x1, mods, g, w_up, wd, dw, dwb, fg, n_latent_tiles, final_norm, layer):
    batch, tokens, _ = x1.shape
    n_tiles = tokens // TILE
    halo_blocks = tokens // FFN_HALO
    per_tile = TILE // FFN_HALO
    kernel = functools.partial(_ffn_kernel, n_latent_tiles=n_latent_tiles, final_norm=final_norm)
    return pl.pallas_call(
        kernel,
        grid=(batch // PAIR, n_tiles),
        in_specs=[
            pl.BlockSpec((PAIR, TILE, D_MODEL), lambda b, t: (b, t, 0)),
            pl.BlockSpec((PAIR, FFN_HALO, D_MODEL),
                         lambda b, t: (b, jnp.maximum(t * per_tile - 1, 0), 0)),
            pl.BlockSpec((PAIR, FFN_HALO, D_MODEL),
                         lambda b, t: (b, jnp.minimum((t + 1) * per_tile, halo_blocks - 1), 0)),
            _mod_spec(n_latent_tiles, batch, PAIR),
            _resident((1, D_MODEL)),
            _layer_weight((D_MODEL, FFN_HIDDEN), layer, 0),
            _layer_weight((D_MODEL, FFN_HIDDEN), layer, 1),
            _layer_weight((FFN_HIDDEN, D_MODEL), layer),
            _resident((FFN_KERNEL, FFN_HIDDEN)),
            _resident((1, FFN_HIDDEN)),
            _resident((1, D_MODEL)),
        ],
        out_specs=pl.BlockSpec((PAIR, TILE, D_MODEL), lambda b, t: (b, t, 0)),
        out_shape=jax.ShapeDtypeStruct((batch, tokens, D_MODEL), F32),
        scratch_shapes=[
            pltpu.VMEM((PAIR, TILE + 2 * FFN_HALO, D_MODEL), F32),
            pltpu.VMEM((PAIR, TILE + 2 * FFN_HALO, FFN_HIDDEN), F32),
            pltpu.VMEM((PAIR * TILE, FFN_HIDDEN), BF16),
        ],
        compiler_params=_params(("parallel", "parallel")),
        name="ffn",
    )(x1, x1, x1, mods, g, w_up, w_up, wd, dw, dwb, fg)


def _rope_tables(n_latent, n_ctx):
    pos = jnp.arange(n_latent, dtype=jnp.int32)
    rc = jnp.stack([pos // GRID_W, pos % GRID_W], axis=0).astype(F32)
    n_freq = HEAD_DIM // 4
    inv_freq = ROPE_THETA ** (-jnp.arange(n_freq, dtype=F32) / n_freq)
    ang = rc[:, None, :] * inv_freq[None, :, None]
    cos, sin = jnp.cos(ang), jnp.sin(ang)
    cos_t = jnp.concatenate([cos, cos], axis=1).reshape(HEAD_DIM, n_latent)
    sin_t = jnp.concatenate([-sin, sin], axis=1).reshape(HEAD_DIM, n_latent)
    cos_t = jnp.concatenate([cos_t, jnp.ones((HEAD_DIM, n_ctx), F32)], axis=1)
    sin_t = jnp.concatenate([sin_t, jnp.zeros((HEAD_DIM, n_ctx), F32)], axis=1)
    return cos_t, sin_t


def kernel(x, c, ctx, c_ctx, w_mod, b_mod, norm1_g, norm2_g, w_in, q_norm_g, k_norm_g,
           conv_dw, conv_dw_b, conv_ln_g, conv_ln_b, w_out, ffn_w_up, ffn_dw, ffn_dw_b,
           ffn_w_down, final_g):
    batch, seq, _ = x.shape
    n_ctx = ctx.shape[1]
    depth = w_mod.shape[0]
    ctx_rows = max(PAIR, PROJ_ELEMS)
    assert seq % TILE == 0 and n_ctx == TILE and batch % ctx_rows == 0 and ctx_rows % PAIR == 0
    assert batch + ctx_rows <= MOD_ROWS and batch % ATTN_ELEMS == 0
    n_latent_tiles = seq // TILE
    n_tiles = n_latent_tiles + 1

    cc = jnp.concatenate([c, jnp.broadcast_to(c_ctx[None, :], (ctx_rows, D_MODEL)),
                          jnp.zeros((MOD_ROWS - batch - ctx_rows, D_MODEL), F32)], axis=0)
    mods = _modulation(cc, w_mod, b_mod).reshape(depth, MOD_ROWS, 6, D_MODEL)
    cos_t, sin_t = _rope_tables(seq, n_ctx)
    x_lat, x_ctx = x, ctx

    w_in_b = w_in.astype(BF16)
    wqkvt = jnp.swapaxes(w_in_b[:, :, :QKV_WIDTH], 1, 2)
    wconv = w_in_b[:, :, QKV_WIDTH:]
    w_out_b = w_out.astype(BF16)
    w_up_b = ffn_w_up.astype(BF16)
    w_down_b = ffn_w_down.astype(BF16)

    for l in range(depth):
        last = l == depth - 1
        gq = jnp.broadcast_to(q_norm_g[l][:, None], (HEAD_DIM, TILE))
        gk = jnp.broadcast_to(k_norm_g[l][:, None], (HEAD_DIM, TILE))
        qt, k, vt, glu = _inproj(x_lat, x_ctx, mods[l], norm1_g[l][None, :], wqkvt, wconv, gq, gk,
                                 cos_t, sin_t, n_latent_tiles, l)
        n_out_tiles = n_latent_tiles if last else n_tiles
        attn = _attention(qt, k, vt, n_latent_tiles, n_out_tiles)
        dw = jnp.concatenate([conv_dw[l], jnp.zeros((1, CONV_WIDTH), F32)], axis=0)
        x1 = _outproj(glu, attn, x_lat, x_ctx, mods[l], dw, conv_dw_b[l][None, :], conv_ln_g[l][None, :],
                      conv_ln_b[l][None, :], w_out_b, n_latent_tiles, n_out_tiles, l)
        x_lat = x_ctx = _ffn(x1, mods[l], norm2_g[l][None, :], w_up_b, w_down_b, ffn_dw[l],
                             ffn_dw_b[l][None, :], final_g[None, :], n_latent_tiles, last, l)
    return x_lat
```

```python
import functools

import jax
import jax.numpy as jnp
from jax import lax
from jax.experimental import pallas as pl
from jax.experimental.pallas import tpu as pltpu

F32 = jnp.float32
BF16 = jnp.bfloat16

D_MODEL = 1024
HEAD_DIM = 64
N_Q_HEADS = 8
N_KV_HEADS = 2
GROUP = N_Q_HEADS // N_KV_HEADS
ATTN_WIDTH = N_Q_HEADS * HEAD_DIM
KV_WIDTH = N_KV_HEADS * HEAD_DIM
QKV_WIDTH = ATTN_WIDTH + 2 * KV_WIDTH
CONV_WIDTH = D_MODEL - ATTN_WIDTH
CONV_KERNEL = 31
CONV_PAD = CONV_KERNEL // 2
FFN_HIDDEN = 2816
FFN_KERNEL = 3
GRID_W = 64
ROPE_THETA = 10000.0
EPS = 1e-6

LANES = 128
SUBLANES = 8
BF16_ROWS = 16
LOG2_E = 1.4426950408889634
Q_SCALE = HEAD_DIM ** -0.5 * LOG2_E
TILE = 256
KEY_BLOCK = 256
PAIR = 2
PROJ_ELEMS = 4
ATTN_ELEMS = 2
CONV_HALO = 16
WIN_STRIDE = 5
FFN_HALO = 8
FFN_CHUNK = 256
MOD_ROWS = 24
MOD_TN = 1024
V7X_VMEM_BYTES = 64 * 1024 * 1024
VMEM_LIMIT = V7X_VMEM_BYTES * 7 // 8

NT_DIMS = (((1,), (1,)), ((), ()))


def _params(sem, vmem=VMEM_LIMIT):
    return pltpu.CompilerParams(dimension_semantics=sem, vmem_limit_bytes=vmem)


def _mod_kernel(c_ref, w_ref, b_ref, o_ref):
    c = c_ref[...]
    s = c * jax.nn.sigmoid(c)
    o_ref[0] = jnp.dot(s, w_ref[0], preferred_element_type=F32,
                       precision=lax.Precision.HIGHEST) + b_ref[0]


def _modulation(cc, w_mod, b_mod):
    depth = w_mod.shape[0]
    n = w_mod.shape[2]
    return pl.pallas_call(
        _mod_kernel,
        grid=(depth, n // MOD_TN),
        in_specs=[
            pl.BlockSpec((MOD_ROWS, D_MODEL), lambda l, j: (0, 0)),
            pl.BlockSpec((1, D_MODEL, MOD_TN), lambda l, j: (l, 0, j)),
            pl.BlockSpec((1, 1, MOD_TN), lambda l, j: (l, 0, j)),
        ],
        out_specs=pl.BlockSpec((1, MOD_ROWS, MOD_TN), lambda l, j: (l, 0, j)),
        out_shape=jax.ShapeDtypeStruct((depth, MOD_ROWS, n), F32),
        compiler_params=_params(("parallel", "parallel")),
        name="modulation",
    )(cc, w_mod, b_mod.reshape(depth, 1, n))


def _rms_mod(x, g, scale, shift):
    ms = jnp.mean(x * x, axis=-1, keepdims=True)
    return (x * lax.rsqrt(ms + EPS)) * g * (1.0 + scale) + shift


def _stream_tile(lat_ref, ctx_ref, n_latent_tiles, e):
    return jnp.where(pl.program_id(1) < n_latent_tiles, lat_ref[e], ctx_ref[e])


def _stream_specs(lat, ctx, n_latent_tiles):
    ctx_block = 0 if ctx.shape[1] == TILE else n_latent_tiles
    return [
        pl.BlockSpec((PROJ_ELEMS, TILE, D_MODEL), lambda b, t: (b, jnp.minimum(t, n_latent_tiles - 1), 0)),
        pl.BlockSpec((PROJ_ELEMS, TILE, D_MODEL), lambda b, t: (b, ctx_block, 0)),
    ]


def _mod_spec(n_latent_tiles, batch, elems):
    return pl.BlockSpec((elems, 6, D_MODEL),
                        lambda b, t: (jnp.where(t < n_latent_tiles, b, batch // elems), 0, 0))


def _resident(shape):
    return pl.BlockSpec(shape, lambda b, t: (0,) * len(shape), pipeline_mode=pl.Buffered(1))


def _layer_weight(shape, layer, col_block=0):
    return pl.BlockSpec((None,) + shape, lambda b, t: (layer, 0, col_block), pipeline_mode=pl.Buffered(1))


def _inproj_kernel(xl_ref, xc_ref, mod_ref, g_ref, wqkvt_ref, wconv_ref, gq_ref, gk_ref, cos_ref, sin_ref,
                   qt_ref, k_ref, vt_ref, glu_ref, *, n_latent_tiles):
    cos = cos_ref[...]
    sin = sin_ref[...]
    gq = gq_ref[...]
    gk = gk_ref[...]

    def project(e):
        h = _rms_mod(_stream_tile(xl_ref, xc_ref, n_latent_tiles, e), g_ref[...],
                     mod_ref[e, 1:2, :], mod_ref[e, 0:1, :]).astype(BF16)
        u = jnp.dot(h, wconv_ref[...], preferred_element_type=F32)
        qkvt = lax.dot_general(wqkvt_ref[...], h, NT_DIMS, preferred_element_type=F32)
        return u, qkvt

    def finish(e, u, qkvt):
        glu_ref[e] = u[:, :CONV_WIDTH] * jax.nn.sigmoid(u[:, CONV_WIDTH:])

        def head(row0, gain, scale):
            blk = qkvt[row0:row0 + HEAD_DIM, :]
            ms = jnp.mean(blk * blk, axis=0, keepdims=True)
            n = blk * lax.rsqrt(ms + EPS) * gain
            q4 = HEAD_DIM // 4
            swapped = jnp.concatenate([n[q4:2 * q4], n[0:q4], n[3 * q4:], n[2 * q4:3 * q4]], axis=0)
            return (n * cos + swapped * sin) * scale

        qt = jnp.concatenate([head(i * HEAD_DIM, gq, Q_SCALE) for i in range(N_Q_HEADS)], axis=0)
        kt = jnp.concatenate([head(ATTN_WIDTH + i * HEAD_DIM, gk, 1.0) for i in range(N_KV_HEADS)], axis=0)
        qt_ref[e] = qt.astype(BF16)
        k_ref[e] = kt.T.astype(BF16)
        vt_ref[e] = qkvt[ATTN_WIDTH + KV_WIDTH:, :].astype(BF16)

    projected = project(0)
    for e in range(PROJ_ELEMS):
        current = projected
        if e + 1 < PROJ_ELEMS:
            projected = project(e + 1)
        finish(e, *current)


def _inproj(x_lat, x_ctx, mods, g, wqkvt, wconv, gq, gk, cos_t, sin_t, n_latent_tiles, layer):
    batch = x_lat.shape[0]
    n_tiles = n_latent_tiles + 1
    tokens = n_tiles * TILE
    return pl.pallas_call(
        functools.partial(_inproj_kernel, n_latent_tiles=n_latent_tiles),
        grid=(batch // PROJ_ELEMS, n_tiles),
        in_specs=_stream_specs(x_lat, x_ctx, n_latent_tiles) + [
            _mod_spec(n_latent_tiles, batch, PROJ_ELEMS),
            _resident((1, D_MODEL)),
            _layer_weight((QKV_WIDTH, D_MODEL), layer),
            _layer_weight((D_MODEL, 2 * CONV_WIDTH), layer),
            _resident((HEAD_DIM, TILE)),
            _resident((HEAD_DIM, TILE)),
            pl.BlockSpec((HEAD_DIM, TILE), lambda b, t: (0, t)),
            pl.BlockSpec((HEAD_DIM, TILE), lambda b, t: (0, t)),
        ],
        out_specs=[
            pl.BlockSpec((PROJ_ELEMS, ATTN_WIDTH, TILE), lambda b, t: (b, 0, t)),
            pl.BlockSpec((PROJ_ELEMS, TILE, KV_WIDTH), lambda b, t: (b, t, 0)),
            pl.BlockSpec((PROJ_ELEMS, KV_WIDTH, TILE), lambda b, t: (b, 0, t)),
            pl.BlockSpec((PROJ_ELEMS, TILE, CONV_WIDTH), lambda b, t: (b, t, 0)),
        ],
        out_shape=[
            jax.ShapeDtypeStruct((batch, ATTN_WIDTH, tokens), BF16),
            jax.ShapeDtypeStruct((batch, tokens, KV_WIDTH), BF16),
            jax.ShapeDtypeStruct((batch, KV_WIDTH, tokens), BF16),
            jax.ShapeDtypeStruct((batch, tokens, CONV_WIDTH), F32),
        ],
        compiler_params=_params(("parallel", "parallel")),
        name="inproj",
    )(x_lat, x_ctx, mods, g, wqkvt, wconv, gq, gk, cos_t, sin_t)


def _key_blocks(key0, key1):
    return [slice(k, min(k + KEY_BLOCK, key1)) for k in range(key0, key1, KEY_BLOCK)]


def _attend(qt_ref, k_ref, vt_ref, o_ref, key0, key1):
    def query_operand(e, g):
        zero = jnp.zeros((HEAD_DIM, TILE), BF16)
        cols = []
        for a in range(GROUP):
            qa = qt_ref[e, (g * GROUP + a) * HEAD_DIM:(g * GROUP + a + 1) * HEAD_DIM, :]
            cols.append(jnp.concatenate([qa if j == g else zero for j in range(N_KV_HEADS)], axis=0))
        return jnp.concatenate(cols, axis=1)

    streams = [(e, g) for e in range(ATTN_ELEMS) for g in range(N_KV_HEADS)]
    w = {st: query_operand(*st) for st in streams}

    blocks = _key_blocks(key0, key1)
    n_blocks = len(blocks)

    def scores(st, i):
        return jnp.dot(k_ref[st[0], blocks[i], :], w[st], preferred_element_type=F32)

    def values(st, i, p):
        e, g = st
        ones = jnp.ones((BF16_ROWS, p.shape[0]), BF16)
        v1 = jnp.concatenate([vt_ref[e, g * HEAD_DIM:(g + 1) * HEAD_DIM, blocks[i]], ones], axis=0)
        return jnp.dot(v1, p, preferred_element_type=F32)

    items = [(st, i) for st in streams for i in range(n_blocks)]
    m, acc, heads = {}, {}, {}

    def fold(st, i, p, alpha):
        v = values(st, i, p)
        acc[st] = v if i == 0 else alpha * acc[st] + v
        if i == n_blocks - 1:
            e, g = st
            ot = acc[st][:HEAD_DIM] / acc[st][HEAD_DIM:HEAD_DIM + 1]
            heads[st] = [ot[:, a * TILE:(a + 1) * TILE] for a in range(GROUP)]
            if g == N_KV_HEADS - 1:
                rows = sum((heads[(e, j)] for j in range(N_KV_HEADS)), [])
                o_ref[e] = jnp.concatenate(rows, axis=0).T.astype(BF16)

    s_next = scores(*items[0])
    pending = None
    for n, (st, i) in enumerate(items):
        s = s_next
        if n + 1 < len(items):
            s_next = scores(*items[n + 1])
        if pending is not None:
            fold(*pending)
        alpha = None
        if i == 0:
            m[st] = jnp.max(s, axis=0, keepdims=True)
        else:
            m_new = jnp.maximum(m[st], jnp.max(s, axis=0, keepdims=True))
            alpha = jnp.exp2(m[st] - m_new)
            m[st] = m_new
        pending = (st, i, jnp.exp2(s - m[st]).astype(BF16), alpha)
    fold(*pending)


def _attn_kernel(qt_ref, k_ref, vt_ref, o_ref, *, n_latent_tiles, n_tokens):
    t = pl.program_id(1)
    n_latent = n_latent_tiles * TILE

    @pl.when(t < n_latent_tiles)
    def _():
        _attend(qt_ref, k_ref, vt_ref, o_ref, 0, n_tokens)

    @pl.when(t >= n_latent_tiles)
    def _():
        _attend(qt_ref, k_ref, vt_ref, o_ref, n_latent, n_tokens)


def _attention(qt, k, vt, n_latent_tiles, n_q_tiles):
    batch, tokens, _ = k.shape
    kernel = functools.partial(_attn_kernel, n_latent_tiles=n_latent_tiles, n_tokens=tokens)
    return pl.pallas_call(
        kernel,
        grid=(batch // ATTN_ELEMS, n_q_tiles),
        in_specs=[
            pl.BlockSpec((ATTN_ELEMS, ATTN_WIDTH, TILE), lambda b, t: (b, 0, t)),
            pl.BlockSpec((ATTN_ELEMS, tokens, KV_WIDTH), lambda b, t: (b, 0, 0)),
            pl.BlockSpec((ATTN_ELEMS, KV_WIDTH, tokens), lambda b, t: (b, 0, 0)),
        ],
        out_specs=pl.BlockSpec((ATTN_ELEMS, TILE, ATTN_WIDTH), lambda b, t: (b, t, 0)),
        out_shape=jax.ShapeDtypeStruct((batch, n_q_tiles * TILE, ATTN_WIDTH), BF16),
        compiler_params=_params(("parallel", "parallel")),
        name="attention",
    )(qt, k, vt)


def _depthwise31(win_ref, dw_ref):
    n_out = TILE // SUBLANES
    n_in = (TILE + 2 * CONV_HALO) // SUBLANES
    taps_a = -(-(CONV_KERNEL + CONV_HALO - CONV_PAD) // SUBLANES)
    cols = []
    for c in range(CONV_WIDTH // LANES):
        lanes = slice(c * LANES, (c + 1) * LANES)
        acc = [None] * n_out
        for i in range(n_in - 1):
            for r in range(SUBLANES):
                piece = win_ref[c, pl.ds(WIN_STRIDE * (i * SUBLANES + r), SUBLANES, stride=WIN_STRIDE), :]
                for a in range(taps_a):
                    o = i - a
                    j = a * SUBLANES + r - (CONV_HALO - CONV_PAD)
                    if 0 <= o < n_out and 0 <= j < CONV_KERNEL:
                        term = piece * dw_ref[j:j + 1, lanes]
                        acc[o] = term if acc[o] is None else acc[o] + term
        cols.append(jnp.concatenate(acc, axis=0))
    return jnp.concatenate(cols, axis=1)


def _outproj_kernel(glu_ref, glu_prev_ref, glu_next_ref, attn_ref, xl_ref, xc_ref, mod_ref, dw_ref, dwb_ref,
                    lng_ref, lnb_ref, wout_ref, o_ref, win_ref, *, n_latent_tiles):
    t = pl.program_id(1)
    has_prev = jnp.logical_and(t != 0, t != n_latent_tiles)
    has_next = jnp.logical_and(t != n_latent_tiles - 1, t != n_latent_tiles)
    for e in range(PROJ_ELEMS):
        win = win_ref.at[e]
        for c in range(CONV_WIDTH // LANES):
            lanes = slice(c * LANES, (c + 1) * LANES)

            def rows(first, n):
                return pl.ds(WIN_STRIDE * first, n, stride=WIN_STRIDE)

            win[c, rows(0, CONV_HALO), :] = jnp.where(has_prev, glu_prev_ref[e, :, lanes], 0.0)
            win[c, rows(CONV_HALO, TILE), :] = glu_ref[e, :, lanes]
            win[c, rows(CONV_HALO + TILE, CONV_HALO), :] = jnp.where(has_next, glu_next_ref[e, :, lanes], 0.0)
        y = _depthwise31(win, dw_ref) + dwb_ref[...]

        mu = jnp.mean(y, axis=-1, keepdims=True)
        d = y - mu
        var = jnp.mean(d * d, axis=-1, keepdims=True)
        z = d * lax.rsqrt(var + EPS) * lng_ref[...] + lnb_ref[...]
        conv = (z * jax.nn.sigmoid(z)).astype(BF16)

        proj = jnp.dot(attn_ref[e], wout_ref[:ATTN_WIDTH, :], preferred_element_type=F32) \
            + jnp.dot(conv, wout_ref[ATTN_WIDTH:, :], preferred_element_type=F32)
        o_ref[e] = _stream_tile(xl_ref, xc_ref, n_latent_tiles, e) + mod_ref[e, 2:3, :] * proj


def _outproj(glu, attn, x_lat, x_ctx, mods, dw, dwb, lng, lnb, wout, n_latent_tiles, n_tiles, layer):
    batch = x_lat.shape[0]
    halo_blocks = glu.shape[1] // CONV_HALO
    per_tile = TILE // CONV_HALO
    kernel = functools.partial(_outproj_kernel, n_latent_tiles=n_latent_tiles)
    return pl.pallas_call(
        kernel,
        grid=(batch // PROJ_ELEMS, n_tiles),
        in_specs=[
            pl.BlockSpec((PROJ_ELEMS, TILE, CONV_WIDTH), lambda b, t: (b, t, 0)),
            pl.BlockSpec((PROJ_ELEMS, CONV_HALO, CONV_WIDTH),
                         lambda b, t: (b, jnp.maximum(t * per_tile - 1, 0), 0)),
            pl.BlockSpec((PROJ_ELEMS, CONV_HALO, CONV_WIDTH),
                         lambda b, t: (b, jnp.minimum((t + 1) * per_tile, halo_blocks - 1), 0)),
            pl.BlockSpec((PROJ_ELEMS, TILE, ATTN_WIDTH), lambda b, t: (b, t, 0)),
        ] + _stream_specs(x_lat, x_ctx, n_latent_tiles) + [
            _mod_spec(n_latent_tiles, batch, PROJ_ELEMS),
            _resident((CONV_KERNEL + 1, CONV_WIDTH)),
            _resident((1, CONV_WIDTH)),
            _resident((1, CONV_WIDTH)),
            _resident((1, CONV_WIDTH)),
            _layer_weight((D_MODEL, D_MODEL), layer),
        ],
        out_specs=pl.BlockSpec((PROJ_ELEMS, TILE, D_MODEL), lambda b, t: (b, t, 0)),
        out_shape=jax.ShapeDtypeStruct((batch, n_tiles * TILE, D_MODEL), F32),
        scratch_shapes=[pltpu.VMEM((PROJ_ELEMS, CONV_WIDTH // LANES, WIN_STRIDE * (TILE + 2 * CONV_HALO), LANES),
                                   F32)],
        compiler_params=_params(("parallel", "parallel")),
        name="outproj",
    )(glu, glu, glu, attn, x_lat, x_ctx, mods, dw, dwb, lng, lnb, wout)


def _ffn_kernel(x_ref, x_prev_ref, x_next_ref, mod_ref, g_ref, wg_ref, wv_ref, wd_ref, dw_ref, dwb_ref,
                fg_ref, o_ref, xw_ref, gate_ref, act_ref, *, n_latent_tiles, final_norm):
    t = pl.program_id(1)
    has_prev = jnp.logical_and(t != 0, t != n_latent_tiles)
    has_next = jnp.logical_and(t != n_latent_tiles - 1, t != n_latent_tiles)
    halo0 = PAIR * TILE

    tiles, halos = [], []
    for e in range(PAIR):
        xw_ref[e, 0:TILE, :] = x_ref[e]
        xw_ref[e, TILE:TILE + FFN_HALO, :] = x_next_ref[e]
        xw_ref[e, TILE + FFN_HALO:, :] = x_prev_ref[e]
        he = _rms_mod(xw_ref[e], g_ref[...], mod_ref[e, 4:5, :], mod_ref[e, 3:4, :]).astype(BF16)
        tiles.append(he[0:TILE])
        halos.append(he[TILE:])
    h = jnp.concatenate(tiles + halos, axis=0)
    h_tiles = h[0:halo0]

    for c in range(FFN_HIDDEN // FFN_CHUNK):
        cols = slice(c * FFN_CHUNK, (c + 1) * FFN_CHUNK)
        gate = jnp.dot(h, wg_ref[:, cols], preferred_element_type=F32)
        val = jnp.dot(h_tiles, wv_ref[:, cols], preferred_element_type=F32)
        for e in range(PAIR):
            nxt = halo0 + 2 * FFN_HALO * e
            gate_ref[e, 0:FFN_HALO, cols] = jnp.where(has_prev, gate[nxt + FFN_HALO:nxt + 2 * FFN_HALO], 0.0)
            gate_ref[e, FFN_HALO:FFN_HALO + TILE, cols] = gate[e * TILE:(e + 1) * TILE]
            gate_ref[e, FFN_HALO + TILE:, cols] = jnp.where(has_next, gate[nxt:nxt + FFN_HALO], 0.0)
            y = dwb_ref[:, cols]
            for j in range(FFN_KERNEL):
                start = FFN_HALO - FFN_KERNEL // 2 + j
                y = y + gate_ref[e, start:start + TILE, cols] * dw_ref[j:j + 1, cols]
            act_ref[e * TILE:(e + 1) * TILE, cols] = \
                (y * jax.nn.sigmoid(y) * val[e * TILE:(e + 1) * TILE]).astype(BF16)

    down = jnp.dot(act_ref[...], wd_ref[...], preferred_element_type=F32)
    for e in range(PAIR):
        out = x_ref[e] + mod_ref[e, 5:6, :] * down[e * TILE:(e + 1) * TILE]
        if final_norm:
            ms = jnp.mean(out * out, axis=-1, keepdims=True)
            out = out * lax.rsqrt(ms + EPS) * fg_ref[...]
        o_ref[e] = out


def _ffn(x1, mods, g, w_up, wd, dw, dwb, fg, n_latent_tiles, final_norm, layer):
    batch, tokens, _ = x1.shape
    n_tiles = tokens // TILE
    halo_blocks = tokens // FFN_HALO
    per_tile = TILE // FFN_HALO
    kernel = functools.partial(_ffn_kernel, n_latent_tiles=n_latent_tiles, final_norm=final_norm)
    return pl.pallas_call(
        kernel,
        grid=(batch // PAIR, n_tiles),
        in_specs=[
            pl.BlockSpec((PAIR, TILE, D_MODEL), lambda b, t: (b, t, 0)),
            pl.BlockSpec((PAIR, FFN_HALO, D_MODEL),
                         lambda b, t: (b, jnp.maximum(t * per_tile - 1, 0), 0)),
            pl.BlockSpec((PAIR, FFN_HALO, D_MODEL),
                         lambda b, t: (b, jnp.minimum((t + 1) * per_tile, halo_blocks - 1), 0)),
            _mod_spec(n_latent_tiles, batch, PAIR),
            _resident((1, D_MODEL)),
            _layer_weight((D_MODEL, FFN_HIDDEN), layer, 0),
            _layer_weight((D_MODEL, FFN_HIDDEN), layer, 1),
            _layer_weight((FFN_HIDDEN, D_MODEL), layer),
            _resident((FFN_KERNEL, FFN_HIDDEN)),
            _resident((1, FFN_HIDDEN)),
            _resident((1, D_MODEL)),
        ],
        out_specs=pl.BlockSpec((PAIR, TILE, D_MODEL), lambda b, t: (b, t, 0)),
        out_shape=jax.ShapeDtypeStruct((batch, tokens, D_MODEL), F32),
        scratch_shapes=[
            pltpu.VMEM((PAIR, TILE + 2 * FFN_HALO, D_MODEL), F32),
            pltpu.VMEM((PAIR, TILE + 2 * FFN_HALO, FFN_HIDDEN), F32),
            pltpu.VMEM((PAIR * TILE, FFN_HIDDEN), BF16),
        ],
        compiler_params=_params(("parallel", "parallel")),
        name="ffn",
    )(x1, x1, x1, mods, g, w_up, w_up, wd, dw, dwb, fg)


def _rope_tables(n_latent, n_ctx):
    pos = jnp.arange(n_latent, dtype=jnp.int32)
    rc = jnp.stack([pos // GRID_W, pos % GRID_W], axis=0).astype(F32)
    n_freq = HEAD_DIM // 4
    inv_freq = ROPE_THETA ** (-jnp.arange(n_freq, dtype=F32) / n_freq)
    ang = rc[:, None, :] * inv_freq[None, :, None]
    cos, sin = jnp.cos(ang), jnp.sin(ang)
    cos_t = jnp.concatenate([cos, cos], axis=1).reshape(HEAD_DIM, n_latent)
    sin_t = jnp.concatenate([-sin, sin], axis=1).reshape(HEAD_DIM, n_latent)
    cos_t = jnp.concatenate([cos_t, jnp.ones((HEAD_DIM, n_ctx), F32)], axis=1)
    sin_t = jnp.concatenate([sin_t, jnp.zeros((HEAD_DIM, n_ctx), F32)], axis=1)
    return cos_t, sin_t


def kernel(x, c, ctx, c_ctx, w_mod, b_mod, norm1_g, norm2_g, w_in, q_norm_g, k_norm_g,
           conv_dw, conv_dw_b, conv_ln_g, conv_ln_b, w_out, ffn_w_up, ffn_dw, ffn_dw_b,
           ffn_w_down, final_g):
    batch, seq, _ = x.shape
    n_ctx = ctx.shape[1]
    depth = w_mod.shape[0]
    ctx_rows = max(PAIR, PROJ_ELEMS)
    assert seq % TILE == 0 and n_ctx == TILE and batch % ctx_rows == 0 and ctx_rows % PAIR == 0
    assert batch + ctx_rows <= MOD_ROWS and batch % ATTN_ELEMS == 0
    n_latent_tiles = seq // TILE
    n_tiles = n_latent_tiles + 1

    cc = jnp.concatenate([c, jnp.broadcast_to(c_ctx[None, :], (ctx_rows, D_MODEL)),
                          jnp.zeros((MOD_ROWS - batch - ctx_rows, D_MODEL), F32)], axis=0)
    mods = _modulation(cc, w_mod, b_mod).reshape(depth, MOD_ROWS, 6, D_MODEL)
    cos_t, sin_t = _rope_tables(seq, n_ctx)
    x_lat, x_ctx = x, ctx

    w_in_b = w_in.astype(BF16)
    wqkvt = jnp.swapaxes(w_in_b[:, :, :QKV_WIDTH], 1, 2)
    wconv = w_in_b[:, :, QKV_WIDTH:]
    w_out_b = w_out.astype(BF16)
    w_up_b = ffn_w_up.astype(BF16)
    w_down_b = ffn_w_down.astype(BF16)

    for l in range(depth):
        last = l == depth - 1
        gq = jnp.broadcast_to(q_norm_g[l][:, None], (HEAD_DIM, TILE))
        gk = jnp.broadcast_to(k_norm_g[l][:, None], (HEAD_DIM, TILE))
        qt, k, vt, glu = _inproj(x_lat, x_ctx, mods[l], norm1_g[l][None, :], wqkvt, wconv, gq, gk,
                                 cos_t, sin_t, n_latent_tiles, l)
        n_out_tiles = n_latent_tiles if last else n_tiles
        attn = _attention(qt, k, vt, n_latent_tiles, n_out_tiles)
        dw = jnp.concatenate([conv_dw[l], jnp.zeros((1, CONV_WIDTH), F32)], axis=0)
        x1 = _outproj(glu, attn, x_lat, x_ctx, mods[l], dw, conv_dw_b[l][None, :], conv_ln_g[l][None, :],
                      conv_ln_b[l][None, :], w_out_b, n_latent_tiles, n_out_tiles, l)
        x_lat = x_ctx = _ffn(x1, mods[l], norm2_g[l][None, :], w_up_b, w_down_b, ffn_dw[l],
                             ffn_dw_b[l][None, :], final_g[None, :], n_latent_tiles, last, l)
    return x_lat
```

```python
import functools

import jax
import jax.numpy as jnp
from jax import lax
from jax.experimental import pallas as pl
from jax.experimental.pallas import tpu as pltpu

F32 = jnp.float32
BF16 = jnp.bfloat16

D_MODEL = 1024
HEAD_DIM = 64
N_Q_HEADS = 8
N_KV_HEADS = 2
GROUP = N_Q_HEADS // N_KV_HEADS
ATTN_WIDTH = N_Q_HEADS * HEAD_DIM
KV_WIDTH = N_KV_HEADS * HEAD_DIM
QKV_WIDTH = ATTN_WIDTH + 2 * KV_WIDTH
CONV_WIDTH = D_MODEL - ATTN_WIDTH
CONV_KERNEL = 31
CONV_PAD = CONV_KERNEL // 2
FFN_HIDDEN = 2816
FFN_KERNEL = 3
GRID_W = 64
ROPE_THETA = 10000.0
EPS = 1e-6

LANES = 128
SUBLANES = 8
BF16_ROWS = 16
LOG2_E = 1.4426950408889634
Q_SCALE = HEAD_DIM ** -0.5 * LOG2_E
TILE = 256
KEY_BLOCK = 256
PAIR = 2
PROJ_ELEMS = 4
ATTN_ELEMS = 2
CONV_HALO = 16
WIN_STRIDE = 5
FFN_HALO = 8
FFN_CHUNK = 256
MOD_ROWS = 24
MOD_TN = 1024
V7X_VMEM_BYTES = 64 * 1024 * 1024
VMEM_LIMIT = V7X_VMEM_BYTES * 7 // 8

NT_DIMS = (((1,), (1,)), ((), ()))


def _params(sem, vmem=VMEM_LIMIT):
    return pltpu.CompilerParams(dimension_semantics=sem, vmem_limit_bytes=vmem)


def _mod_kernel(c_ref, w_ref, b_ref, o_ref):
    c = c_ref[...]
    s = c * jax.nn.sigmoid(c)
    o_ref[0] = jnp.dot(s, w_ref[0], preferred_element_type=F32,
                       precision=lax.Precision.HIGHEST) + b_ref[0]


def _modulation(cc, w_mod, b_mod):
    depth = w_mod.shape[0]
    n = w_mod.shape[2]
    return pl.pallas_call(
        _mod_kernel,
        grid=(depth, n // MOD_TN),
        in_specs=[
            pl.BlockSpec((MOD_ROWS, D_MODEL), lambda l, j: (0, 0)),
            pl.BlockSpec((1, D_MODEL, MOD_TN), lambda l, j: (l, 0, j)),
            pl.BlockSpec((1, 1, MOD_TN), lambda l, j: (l, 0, j)),
        ],
        out_specs=pl.BlockSpec((1, MOD_ROWS, MOD_TN), lambda l, j: (l, 0, j)),
        out_shape=jax.ShapeDtypeStruct((depth, MOD_ROWS, n), F32),
        compiler_params=_params(("parallel", "parallel")),
        name="modulation",
    )(cc, w_mod, b_mod.reshape(depth, 1, n))


def _rms_mod(x, g, scale, shift):
    ms = jnp.mean(x * x, axis=-1, keepdims=True)
    return (x * lax.rsqrt(ms + EPS)) * g * (1.0 + scale) + shift


def _stream_tile(lat_ref, ctx_ref, n_latent_tiles, e):
    return jnp.where(pl.program_id(1) < n_latent_tiles, lat_ref[e], ctx_ref[e])


def _stream_specs(lat, ctx, n_latent_tiles):
    ctx_block = 0 if ctx.shape[1] == TILE else n_latent_tiles
    return [
        pl.BlockSpec((PROJ_ELEMS, TILE, D_MODEL), lambda b, t: (b, jnp.minimum(t, n_latent_tiles - 1), 0)),
        pl.BlockSpec((PROJ_ELEMS, TILE, D_MODEL), lambda b, t: (b, ctx_block, 0)),
    ]


def _mod_spec(n_latent_tiles, batch, elems):
    return pl.BlockSpec((elems, 6, D_MODEL),
                        lambda b, t: (jnp.where(t < n_latent_tiles, b, batch // elems), 0, 0))


def _resident(shape):
    return pl.BlockSpec(shape, lambda b, t: (0,) * len(shape), pipeline_mode=pl.Buffered(1))


def _layer_weight(shape, layer, col_block=0):
    return pl.BlockSpec((None,) + shape, lambda b, t: (layer, 0, col_block), pipeline_mode=pl.Buffered(1))


def _inproj_kernel(xl_ref, xc_ref, mod_ref, g_ref, wqkvt_ref, wconv_ref, gq_ref, gk_ref, cos_ref, sin_ref,
                   qt_ref, k_ref, vt_ref, glu_ref, *, n_latent_tiles):
    cos = cos_ref[...]
    sin = sin_ref[...]
    gq = gq_ref[...]
    gk = gk_ref[...]

    def project(e):
        h = _rms_mod(_stream_tile(xl_ref, xc_ref, n_latent_tiles, e), g_ref[...],
                     mod_ref[e, 1:2, :], mod_ref[e, 0:1, :]).astype(BF16)
        u = jnp.dot(h, wconv_ref[...], preferred_element_type=F32)
        qkvt = lax.dot_general(wqkvt_ref[...], h, NT_DIMS, preferred_element_type=F32)
        return u, qkvt

    def finish(e, u, qkvt):
        glu_ref[e] = u[:, :CONV_WIDTH] * jax.nn.sigmoid(u[:, CONV_WIDTH:])

        def head(row0, gain, scale):
            blk = qkvt[row0:row0 + HEAD_DIM, :]
            ms = jnp.mean(blk * blk, axis=0, keepdims=True)
            n = blk * lax.rsqrt(ms + EPS) * gain
            q4 = HEAD_DIM // 4
            swapped = jnp.concatenate([n[q4:2 * q4], n[0:q4], n[3 * q4:], n[2 * q4:3 * q4]], axis=0)
            return (n * cos + swapped * sin) * scale

        qt = jnp.concatenate([head(i * HEAD_DIM, gq, Q_SCALE) for i in range(N_Q_HEADS)], axis=0)
        kt = jnp.concatenate([head(ATTN_WIDTH + i * HEAD_DIM, gk, 1.0) for i in range(N_KV_HEADS)], axis=0)
        qt_ref[e] = qt.astype(BF16)
        k_ref[e] = kt.T.astype(BF16)
        vt_ref[e] = qkvt[ATTN_WIDTH + KV_WIDTH:, :].astype(BF16)

    projected = project(0)
    for e in range(PROJ_ELEMS):
        current = projected
        if e + 1 < PROJ_ELEMS:
            projected = project(e + 1)
        finish(e, *current)


def _inproj(x_lat, x_ctx, mods, g, wqkvt, wconv, gq, gk, cos_t, sin_t, n_latent_tiles, layer):
    batch = x_lat.shape[0]
    n_tiles = n_latent_tiles + 1
    tokens = n_tiles * TILE
    return pl.pallas_call(
        functools.partial(_inproj_kernel, n_latent_tiles=n_latent_tiles),
        grid=(batch // PROJ_ELEMS, n_tiles),
        in_specs=_stream_specs(x_lat, x_ctx, n_latent_tiles) + [
            _mod_spec(n_latent_tiles, batch, PROJ_ELEMS),
            _resident((1, D_MODEL)),
            _layer_weight((QKV_WIDTH, D_MODEL), layer),
            _layer_weight((D_MODEL, 2 * CONV_WIDTH), layer),
            _resident((HEAD_DIM, TILE)),
            _resident((HEAD_DIM, TILE)),
            pl.BlockSpec((HEAD_DIM, TILE), lambda b, t: (0, t)),
            pl.BlockSpec((HEAD_DIM, TILE), lambda b, t: (0, t)),
        ],
        out_specs=[
            pl.BlockSpec((PROJ_ELEMS, ATTN_WIDTH, TILE), lambda b, t: (b, 0, t)),
            pl.BlockSpec((PROJ_ELEMS, TILE, KV_WIDTH), lambda b, t: (b, t, 0)),
            pl.BlockSpec((PROJ_ELEMS, KV_WIDTH, TILE), lambda b, t: (b, 0, t)),
            pl.BlockSpec((PROJ_ELEMS, TILE, CONV_WIDTH), lambda b, t: (b, t, 0)),
        ],
        out_shape=[
            jax.ShapeDtypeStruct((batch, ATTN_WIDTH, tokens), BF16),
            jax.ShapeDtypeStruct((batch, tokens, KV_WIDTH), BF16),
            jax.ShapeDtypeStruct((batch, KV_WIDTH, tokens), BF16),
            jax.ShapeDtypeStruct((batch, tokens, CONV_WIDTH), F32),
        ],
        compiler_params=_params(("parallel", "parallel")),
        name="inproj",
    )(x_lat, x_ctx, mods, g, wqkvt, wconv, gq, gk, cos_t, sin_t)


def _key_blocks(key0, key1):
    return [slice(k, min(k + KEY_BLOCK, key1)) for k in range(key0, key1, KEY_BLOCK)]


def _attend(qt_ref, k_ref, vt_ref, o_ref, key0, key1):
    def query_operand(e, g):
        zero = jnp.zeros((HEAD_DIM, TILE), BF16)
        cols = []
        for a in range(GROUP):
            qa = qt_ref[e, (g * GROUP + a) * HEAD_DIM:(g * GROUP + a + 1) * HEAD_DIM, :]
            cols.append(jnp.concatenate([qa if j == g else zero for j in range(N_KV_HEADS)], axis=0))
        return jnp.concatenate(cols, axis=1)

    streams = [(e, g) for e in range(ATTN_ELEMS) for g in range(N_KV_HEADS)]
    w = {st: query_operand(*st) for st in streams}

    blocks = _key_blocks(key0, key1)
    n_blocks = len(blocks)

    def scores(st, i):
        return jnp.dot(k_ref[st[0], blocks[i], :], w[st], preferred_element_type=F32)

    def values(st, i, p):
        e, g = st
        ones = jnp.ones((BF16_ROWS, p.shape[0]), BF16)
        v1 = jnp.concatenate([vt_ref[e, g * HEAD_DIM:(g + 1) * HEAD_DIM, blocks[i]], ones], axis=0)
        return jnp.dot(v1, p, preferred_element_type=F32)

    items = [(st, i) for st in streams for i in range(n_blocks)]
    m, acc, heads = {}, {}, {}

    def fold(st, i, p, alpha):
        v = values(st, i, p)
        acc[st] = v if i == 0 else alpha * acc[st] + v
        if i == n_blocks - 1:
            e, g = st
            ot = acc[st][:HEAD_DIM] / acc[st][HEAD_DIM:HEAD_DIM + 1]
            heads[st] = [ot[:, a * TILE:(a + 1) * TILE] for a in range(GROUP)]
            if g == N_KV_HEADS - 1:
                rows = sum((heads[(e, j)] for j in range(N_KV_HEADS)), [])
                o_ref[e] = jnp.concatenate(rows, axis=0).T.astype(BF16)

    s_next = scores(*items[0])
    pending = None
    for n, (st, i) in enumerate(items):
        s = s_next
        if n + 1 < len(items):
            s_next = scores(*items[n + 1])
        if pending is not None:
            fold(*pending)
        alpha = None
        if i == 0:
            m[st] = jnp.max(s, axis=0, keepdims=True)
        else:
            m_new = jnp.maximum(m[st], jnp.max(s, axis=0, keepdims=True))
            alpha = jnp.exp2(m[st] - m_new)
            m[st] = m_new
        pending = (st, i, jnp.exp2(s - m[st]).astype(BF16), alpha)
    fold(*pending)


def _attn_kernel(qt_ref, k_ref, vt_ref, o_ref, *, n_latent_tiles, n_tokens):
    t = pl.program_id(1)
    n_latent = n_latent_tiles * TILE

    @pl.when(t < n_latent_tiles)
    def _():
        _attend(qt_ref, k_ref, vt_ref, o_ref, 0, n_tokens)

    @pl.when(t >= n_latent_tiles)
    def _():
        _attend(qt_ref, k_ref, vt_ref, o_ref, n_latent, n_tokens)


def _attention(qt, k, vt, n_latent_tiles, n_q_tiles):
    batch, tokens, _ = k.shape
    kernel = functools.partial(_attn_kernel, n_latent_tiles=n_latent_tiles, n_tokens=tokens)
    return pl.pallas_call(
        kernel,
        grid=(batch // ATTN_ELEMS, n_q_tiles),
        in_specs=[
            pl.BlockSpec((ATTN_ELEMS, ATTN_WIDTH, TILE), lambda b, t: (b, 0, t)),
            pl.BlockSpec((ATTN_ELEMS, tokens, KV_WIDTH), lambda b, t: (b, 0, 0)),
            pl.BlockSpec((ATTN_ELEMS, KV_WIDTH, tokens), lambda b, t: (b, 0, 0)),
        ],
        out_specs=pl.BlockSpec((ATTN_ELEMS, TILE, ATTN_WIDTH), lambda b, t: (b, t, 0)),
        out_shape=jax.ShapeDtypeStruct((batch, n_q_tiles * TILE, ATTN_WIDTH), BF16),
        compiler_params=_params(("parallel", "parallel")),
        name="attention",
    )(qt, k, vt)


def _depthwise31(win_ref, dw_ref):
    n_out = TILE // SUBLANES
    n_in = (TILE + 2 * CONV_HALO) // SUBLANES
    taps_a = -(-(CONV_KERNEL + CONV_HALO - CONV_PAD) // SUBLANES)
    cols = []
    for c in range(CONV_WIDTH // LANES):
        lanes = slice(c * LANES, (c + 1) * LANES)
        acc = [None] * n_out
        for i in range(n_in - 1):
            for r in range(SUBLANES):
                piece = win_ref[c, pl.ds(WIN_STRIDE * (i * SUBLANES + r), SUBLANES, stride=WIN_STRIDE), :]
                for a in range(taps_a):
                    o = i - a
                    j = a * SUBLANES + r - (CONV_HALO - CONV_PAD)
                    if 0 <= o < n_out and 0 <= j < CONV_KERNEL:
                        term = piece * dw_ref[j:j + 1, lanes]
                        acc[o] = term if acc[o] is None else acc[o] + term
        cols.append(jnp.concatenate(acc, axis=0))
    return jnp.concatenate(cols, axis=1)


def _outproj_kernel(glu_ref, glu_prev_ref, glu_next_ref, attn_ref, xl_ref, xc_ref, mod_ref, dw_ref, dwb_ref,
                    lng_ref, lnb_ref, wout_ref, o_ref, win_ref, *, n_latent_tiles):
    t = pl.program_id(1)
    has_prev = jnp.logical_and(t != 0, t != n_latent_tiles)
    has_next = jnp.logical_and(t != n_latent_tiles - 1, t != n_latent_tiles)
    for e in range(PROJ_ELEMS):
        win = win_ref.at[e]
        for c in range(CONV_WIDTH // LANES):
            lanes = slice(c * LANES, (c + 1) * LANES)

            def rows(first, n):
                return pl.ds(WIN_STRIDE * first, n, stride=WIN_STRIDE)

            win[c, rows(0, CONV_HALO), :] = jnp.where(has_prev, glu_prev_ref[e, :, lanes], 0.0)
            win[c, rows(CONV_HALO, TILE), :] = glu_ref[e, :, lanes]
            win[c, rows(CONV_HALO + TILE, CONV_HALO), :] = jnp.where(has_next, glu_next_ref[e, :, lanes], 0.0)
        y = _depthwise31(win, dw_ref) + dwb_ref[...]

        mu = jnp.mean(y, axis=-1, keepdims=True)
        d = y - mu
        var = jnp.mean(d * d, axis=-1, keepdims=True)
        z = d * lax.rsqrt(var + EPS) * lng_ref[...] + lnb_ref[...]
        conv = (z * jax.nn.sigmoid(z)).astype(BF16)

        proj = jnp.dot(attn_ref[e], wout_ref[:ATTN_WIDTH, :], preferred_element_type=F32) \
            + jnp.dot(conv, wout_ref[ATTN_WIDTH:, :], preferred_element_type=F32)
        o_ref[e] = _stream_tile(xl_ref, xc_ref, n_latent_tiles, e) + mod_ref[e, 2:3, :] * proj


def _outproj(glu, attn, x_lat, x_ctx, mods, dw, dwb, lng, lnb, wout, n_latent_tiles, n_tiles, layer):
    batch = x_lat.shape[0]
    halo_blocks = glu.shape[1] // CONV_HALO
    per_tile = TILE // CONV_HALO
    kernel = functools.partial(_outproj_kernel, n_latent_tiles=n_latent_tiles)
    return pl.pallas_call(
        kernel,
        grid=(batch // PROJ_ELEMS, n_tiles),
        in_specs=[
            pl.BlockSpec((PROJ_ELEMS, TILE, CONV_WIDTH), lambda b, t: (b, t, 0)),
            pl.BlockSpec((PROJ_ELEMS, CONV_HALO, CONV_WIDTH),
                         lambda b, t: (b, jnp.maximum(t * per_tile - 1, 0), 0)),
            pl.BlockSpec((PROJ_ELEMS, CONV_HALO, CONV_WIDTH),
                         lambda b, t: (b, jnp.minimum((t + 1) * per_tile, halo_blocks - 1), 0)),
            pl.BlockSpec((PROJ_ELEMS, TILE, ATTN_WIDTH), lambda b, t: (b, t, 0)),
        ] + _stream_specs(x_lat, x_ctx, n_latent_tiles) + [
            _mod_spec(n_latent_tiles, batch, PROJ_ELEMS),
            _resident((CONV_KERNEL + 1, CONV_WIDTH)),
            _resident((1, CONV_WIDTH)),
            _resident((1, CONV_WIDTH)),
            _resident((1, CONV_WIDTH)),
            _layer_weight((D_MODEL, D_MODEL), layer),
        ],
        out_specs=pl.BlockSpec((PROJ_ELEMS, TILE, D_MODEL), lambda b, t: (b, t, 0)),
        out_shape=jax.ShapeDtypeStruct((batch, n_tiles * TILE, D_MODEL), F32),
        scratch_shapes=[pltpu.VMEM((PROJ_ELEMS, CONV_WIDTH // LANES, WIN_STRIDE * (TILE + 2 * CONV_HALO), LANES),
                                   F32)],
        compiler_params=_params(("parallel", "parallel")),
        name="outproj",
    )(glu, glu, glu, attn, x_lat, x_ctx, mods, dw, dwb, lng, lnb, wout)


def _ffn_kernel(x_ref, x_prev_ref, x_next_ref, mod_ref, g_ref, wg_ref, wv_ref, wd_ref, dw_ref, dwb_ref,
                fg_ref, o_ref, xw_ref, gate_ref, act_ref, *, n_latent_tiles, final_norm):
    t = pl.program_id(1)
    has_prev = jnp.logical_and(t != 0, t != n_latent_tiles)
    has_next = jnp.logical_and(t != n_latent_tiles - 1, t != n_latent_tiles)
    halo0 = PAIR * TILE

    tiles, halos = [], []
    for e in range(PAIR):
        xw_ref[e, 0:TILE, :] = x_ref[e]
        xw_ref[e, TILE:TILE + FFN_HALO, :] = x_next_ref[e]
        xw_ref[e, TILE + FFN_HALO:, :] = x_prev_ref[e]
        he = _rms_mod(xw_ref[e], g_ref[...], mod_ref[e, 4:5, :], mod_ref[e, 3:4, :]).astype(BF16)
        tiles.append(he[0:TILE])
        halos.append(he[TILE:])
    h = jnp.concatenate(tiles + halos, axis=0)
    h_tiles = h[0:halo0]

    for c in range(FFN_HIDDEN // FFN_CHUNK):
        cols = slice(c * FFN_CHUNK, (c + 1) * FFN_CHUNK)
        gate = jnp.dot(h, wg_ref[:, cols], preferred_element_type=F32)
        val = jnp.dot(h_tiles, wv_ref[:, cols], preferred_element_type=F32)
        for e in range(PAIR):
            nxt = halo0 + 2 * FFN_HALO * e
            ys = []
            for k in range(FFN_CHUNK // LANES):
                slab = c * (FFN_CHUNK // LANES) + k
                lanes = slice(k * LANES, (k + 1) * LANES)

                def rows(first, n):
                    return pl.ds(3 * first, n, stride=3)

                gate_ref[e, slab, rows(0, FFN_HALO), :] = jnp.where(
                    has_prev, gate[nxt + FFN_HALO:nxt + 2 * FFN_HALO, lanes], 0.0)
                gate_ref[e, slab, rows(FFN_HALO, TILE), :] = gate[e * TILE:(e + 1) * TILE, lanes]
                gate_ref[e, slab, rows(FFN_HALO + TILE, FFN_HALO), :] = jnp.where(
                    has_next, gate[nxt:nxt + FFN_HALO, lanes], 0.0)
                yk = dwb_ref[:, slab * LANES:(slab + 1) * LANES]
                for j in range(FFN_KERNEL):
                    start = FFN_HALO - FFN_KERNEL // 2 + j
                    yk = yk + gate_ref[e, slab, rows(start, TILE), :] \
                        * dw_ref[j:j + 1, slab * LANES:(slab + 1) * LANES]
                ys.append(yk)
            y = jnp.concatenate(ys, axis=1)
            act_ref[e * TILE:(e + 1) * TILE, cols] = \
                (y * jax.nn.sigmoid(y) * val[e * TILE:(e + 1) * TILE]).astype(BF16)

    down = jnp.dot(act_ref[...], wd_ref[...], preferred_element_type=F32)
    for e in range(PAIR):
        out = x_ref[e] + mod_ref[e, 5:6, :] * down[e * TILE:(e + 1) * TILE]
        if final_norm:
            ms = jnp.mean(out * out, axis=-1, keepdims=True)
            out = out * lax.rsqrt(ms + EPS) * fg_ref[...]
        o_ref[e] = out


def _ffn(x1, mods, g, w_up, wd, dw, dwb, fg, n_latent_tiles, final_norm, layer):
    batch, tokens, _ = x1.shape
    n_tiles = tokens // TILE
    halo_blocks = tokens // FFN_HALO
    per_tile = TILE // FFN_HALO
    kernel = functools.partial(_ffn_kernel, n_latent_tiles=n_latent_tiles, final_norm=final_norm)
    return pl.pallas_call(
        kernel,
        grid=(batch // PAIR, n_tiles),
        in_specs=[
            pl.BlockSpec((PAIR, TILE, D_MODEL), lambda b, t: (b, t, 0)),
            pl.BlockSpec((PAIR, FFN_HALO, D_MODEL),
                         lambda b, t: (b, jnp.maximum(t * per_tile - 1, 0), 0)),
            pl.BlockSpec((PAIR, FFN_HALO, D_MODEL),
                         lambda b, t: (b, jnp.minimum((t + 1) * per_tile, halo_blocks - 1), 0)),
            _mod_spec(n_latent_tiles, batch, PAIR),
            _resident((1, D_MODEL)),
            _layer_weight((D_MODEL, FFN_HIDDEN), layer, 0),
            _layer_weight((D_MODEL, FFN_HIDDEN), layer, 1),
            _layer_weight((FFN_HIDDEN, D_MODEL), layer),
            _resident((FFN_KERNEL, FFN_HIDDEN)),
            _resident((1, FFN_HIDDEN)),
            _resident((1, D_MODEL)),
        ],
        out_specs=pl.BlockSpec((PAIR, TILE, D_MODEL), lambda b, t: (b, t, 0)),
        out_shape=jax.ShapeDtypeStruct((batch, tokens, D_MODEL), F32),
        scratch_shapes=[
            pltpu.VMEM((PAIR, TILE + 2 * FFN_HALO, D_MODEL), F32),
            pltpu.VMEM((PAIR, FFN_HIDDEN // LANES, 3 * (TILE + 2 * FFN_HALO), LANES), F32),
            pltpu.VMEM((PAIR * TILE, FFN_HIDDEN), BF16),
        ],
        compiler_params=_params(("parallel", "parallel")),
        name="ffn",
    )(x1, x1, x1, mods, g, w_up, w_up, wd, dw, dwb, fg)


def _rope_tables(n_latent, n_ctx):
    pos = jnp.arange(n_latent, dtype=jnp.int32)
    rc = jnp.stack([pos // GRID_W, pos % GRID_W], axis=0).astype(F32)
    n_freq = HEAD_DIM // 4
    inv_freq = ROPE_THETA ** (-jnp.arange(n_freq, dtype=F32) / n_freq)
    ang = rc[:, None, :] * inv_freq[None, :, None]
    cos, sin = jnp.cos(ang), jnp.sin(ang)
    cos_t = jnp.concatenate([cos, cos], axis=1).reshape(HEAD_DIM, n_latent)
    sin_t = jnp.concatenate([-sin, sin], axis=1).reshape(HEAD_DIM, n_latent)
    cos_t = jnp.concatenate([cos_t, jnp.ones((HEAD_DIM, n_ctx), F32)], axis=1)
    sin_t = jnp.concatenate([sin_t, jnp.zeros((HEAD_DIM, n_ctx), F32)], axis=1)
    return cos_t, sin_t


def kernel(x, c, ctx, c_ctx, w_mod, b_mod, norm1_g, norm2_g, w_in, q_norm_g, k_norm_g,
           conv_dw, conv_dw_b, conv_ln_g, conv_ln_b, w_out, ffn_w_up, ffn_dw, ffn_dw_b,
           ffn_w_down, final_g):
    batch, seq, _ = x.shape
    n_ctx = ctx.shape[1]
    depth = w_mod.shape[0]
    ctx_rows = max(PAIR, PROJ_ELEMS)
    assert seq % TILE == 0 and n_ctx == TILE and batch % ctx_rows == 0 and ctx_rows % PAIR == 0
    assert batch + ctx_rows <= MOD_ROWS and batch % ATTN_ELEMS == 0
    n_latent_tiles = seq // TILE
    n_tiles = n_latent_tiles + 1

    cc = jnp.concatenate([c, jnp.broadcast_to(c_ctx[None, :], (ctx_rows, D_MODEL)),
                          jnp.zeros((MOD_ROWS - batch - ctx_rows, D_MODEL), F32)], axis=0)
    mods = _modulation(cc, w_mod, b_mod).reshape(depth, MOD_ROWS, 6, D_MODEL)
    cos_t, sin_t = _rope_tables(seq, n_ctx)
    x_lat, x_ctx = x, ctx

    w_in_b = w_in.astype(BF16)
    wqkvt = jnp.swapaxes(w_in_b[:, :, :QKV_WIDTH], 1, 2)
    wconv = w_in_b[:, :, QKV_WIDTH:]
    w_out_b = w_out.astype(BF16)
    w_up_b = ffn_w_up.astype(BF16)
    w_down_b = ffn_w_down.astype(BF16)

    for l in range(depth):
        last = l == depth - 1
        gq = jnp.broadcast_to(q_norm_g[l][:, None], (HEAD_DIM, TILE))
        gk = jnp.broadcast_to(k_norm_g[l][:, None], (HEAD_DIM, TILE))
        qt, k, vt, glu = _inproj(x_lat, x_ctx, mods[l], norm1_g[l][None, :], wqkvt, wconv, gq, gk,
                                 cos_t, sin_t, n_latent_tiles, l)
        n_out_tiles = n_latent_tiles if last else n_tiles
        attn = _attention(qt, k, vt, n_latent_tiles, n_out_tiles)
        dw = jnp.concatenate([conv_dw[l], jnp.zeros((1, CONV_WIDTH), F32)], axis=0)
        x1 = _outproj(glu, attn, x_lat, x_ctx, mods[l], dw, conv_dw_b[l][None, :], conv_ln_g[l][None, :],
                      conv_ln_b[l][None, :], w_out_b, n_latent_tiles, n_out_tiles, l)
        x_lat = x_ctx = _ffn(x1, mods[l], norm2_g[l][None, :], w_up_b, w_down_b, ffn_dw[l],
                             ffn_dw_b[l][None, :], final_g[None, :], n_latent_tiles, last, l)
    return x_lat
```
